```python
import math
import jax
import jax.numpy as jnp
from jax import lax
import numpy as np

D_MODEL = 2048
BATCH = 2
SEQ = 4096
DEPTH = 2
DEC_BATCH = 8
DEC_SEQ = 16
PAST_LEN = 2048

CHUNK = 64
EPS = 1e-6

SSM_INNER = D_MODEL
SSM_HEAD_DIM = 64
SSM_HEADS = SSM_INNER // SSM_HEAD_DIM
SSM_GROUPS = 4
SSM_STATE = 128
SSM_CONV = 4
SSM_CONV_DIM = SSM_INNER + 2 * SSM_GROUPS * SSM_STATE
SSM_BLOCK = CHUNK
DT_MIN = 1e-3
DT_MAX = 1e-1

SC_DIM = D_MODEL
SC_WIDTH = 3

N_HEADS = 16
HEAD_DIM = D_MODEL // N_HEADS
ATT_DIM = N_HEADS * HEAD_DIM
ATT_PAST_CHUNKS = 8
ATT_PAST = ATT_PAST_CHUNKS * CHUNK
ATT_BAND = (ATT_PAST_CHUNKS + 1) * CHUNK
MAX_REL = 128
N_REL = 2 * MAX_REL + 1

N_BRANCH = 3
D_FF = 4 * D_MODEL

IN_SPLITS = (SSM_INNER, SSM_CONV_DIM, SSM_HEADS, SC_DIM, SC_DIM, SC_DIM, ATT_DIM, ATT_DIM, ATT_DIM, N_BRANCH * D_MODEL)
IN_COLS = sum(IN_SPLITS)
IN_OFFSETS = tuple(int(o) for o in np.cumsum(IN_SPLITS)[:-1])

NEG_INF = -1e30

kernel_name = 'hybrid_stream_ssd_conv_chunkattn_step'


def _rmsnorm(x, g):
    xf = x.astype(jnp.float32)
    y = xf * lax.rsqrt(jnp.mean(xf * xf, axis=-1, keepdims=True) + EPS)
    return (y * g.astype(jnp.float32)).astype(x.dtype)


def _causal_dwconv(u, prefix, w, b=None):
    cat = jnp.concatenate([prefix.astype(u.dtype), u], axis=1)
    out = lax.conv_general_dilated(cat, w[:, None, :].astype(u.dtype), window_strides=(1,), padding='VALID',
                                   dimension_numbers=('NWC', 'WIO', 'NWC'), feature_group_count=u.shape[-1])
    if b is not None:
        out = out + b.astype(out.dtype)
    return out, cat[:, -(w.shape[0] - 1):]


def _ssd(x, dt, a, bm, cm, h0, block):
    f32 = jnp.float32
    b_, L = x.shape[:2]
    nc = L // block
    r = SSM_HEADS // SSM_GROUPS
    xs = (x.astype(f32) * dt[..., None]).reshape(b_, nc, block, SSM_GROUPS, r, SSM_HEAD_DIM)
    da = (dt * a).reshape(b_, nc, block, SSM_GROUPS, r)
    bm = bm.astype(f32).reshape(b_, nc, block, SSM_GROUPS, SSM_STATE)
    cm = cm.astype(f32).reshape(b_, nc, block, SSM_GROUPS, SSM_STATE)
    acum = jnp.cumsum(da, axis=2)
    diff = acum[:, :, :, None] - acum[:, :, None, :]
    causal = jnp.tril(jnp.ones((block, block), bool))[:, :, None, None]
    decay = jnp.exp(jnp.where(causal, diff, -jnp.inf))
    cb = jnp.einsum('bclgn,bcsgn->bclsg', cm, bm)
    y_diag = jnp.einsum('bclsg,bclsgr,bcsgrp->bclgrp', cb, decay, xs)
    decay_end = jnp.exp(acum[:, :, -1:] - acum)
    states = jnp.einsum('bclgn,bclgr,bclgrp->bcgrpn', bm, decay_end, xs)
    block_decay = jnp.exp(acum[:, :, -1])

    def step(h, inp):
        s, d = inp
        return h * d[..., None, None] + s, h

    h0g = h0.astype(f32).reshape(b_, SSM_GROUPS, r, SSM_HEAD_DIM, SSM_STATE)
    h_last, h_in = lax.scan(step, h0g, (jnp.moveaxis(states, 1, 0), jnp.moveaxis(block_decay, 1, 0)))
    h_in = jnp.moveaxis(h_in, 0, 1)
    y_off = jnp.einsum('bclgn,bcgrpn,bclgr->bclgrp', cm, h_in, jnp.exp(acum))
    y = (y_diag + y_off).reshape(b_, L, SSM_HEADS, SSM_HEAD_DIM)
    return y, h_last.reshape(b_, SSM_HEADS, SSM_HEAD_DIM, SSM_STATE)


def _ssd_mixer(z, xbc, dt_raw, conv_prefix, h0, lp, block):
    f32 = jnp.float32
    b_, L = z.shape[:2]
    xbc, conv_state = _causal_dwconv(xbc, conv_prefix, lp['ssm_conv_w'], lp['ssm_conv_b'])
    xbc = jax.nn.silu(xbc)
    xh, bm, cm = jnp.split(xbc, [SSM_INNER, SSM_INNER + SSM_GROUPS * SSM_STATE], axis=-1)
    xh = xh.reshape(b_, L, SSM_HEADS, SSM_HEAD_DIM)
    bm = bm.reshape(b_, L, SSM_GROUPS, SSM_STATE)
    cm = cm.reshape(b_, L, SSM_GROUPS, SSM_STATE)
    dt = jax.nn.softplus(dt_raw.astype(f32) + lp['ssm_dt_bias'].astype(f32))
    a = -jnp.exp(lp['ssm_a_log'].astype(f32))
    y, h_last = _ssd(xh, dt, a, bm, cm, h0, block)
    y = y + lp['ssm_d'].astype(f32)[:, None] * xh.astype(f32)
    y = y.reshape(b_, L, SSM_INNER) * jax.nn.silu(z.astype(f32))
    yg = y.reshape(b_, L, SSM_GROUPS, SSM_INNER // SSM_GROUPS)
    yg = yg * lax.rsqrt(jnp.mean(yg * yg, axis=-1, keepdims=True) + EPS)
    y = (yg.reshape(b_, L, SSM_INNER) * lp['ssm_norm_g'].astype(f32)).astype(z.dtype)
    return y @ lp['ssm_out_w'], h_last, conv_state


def _short_conv_mixer(bg, cg, hx, prefix, lp):
    u = cg * hx
    v, state = _causal_dwconv(u, prefix, lp['sc_conv_w'])
    return (bg * v) @ lp['sc_out_w'], state


def _band_attention(q, k, v, rel, valid, rel_bias):
    f32 = jnp.float32
    s = jnp.einsum('...qhd,...khd->...hqk', q.astype(f32), k.astype(f32)) * (HEAD_DIM ** -0.5)
    bias = rel_bias.astype(f32)[:, jnp.clip(rel, -MAX_REL, MAX_REL) + MAX_REL]
    s = jnp.where(valid, s + bias, NEG_INF)
    p = jax.nn.softmax(s, axis=-1)
    return jnp.einsum('...hqk,...khd->...qhd', p, v.astype(f32)).astype(q.dtype)


def _attn_prompt(q, k, v, rel_bias):
    b_, L = q.shape[:2]
    nc = L // CHUNK
    qc = q.reshape(b_, nc, CHUNK, N_HEADS, HEAD_DIM)
    pad = ((0, 0), (ATT_PAST_CHUNKS, 0), (0, 0), (0, 0), (0, 0))
    kc = jnp.pad(k.reshape(b_, nc, CHUNK, N_HEADS, HEAD_DIM), pad)
    vc = jnp.pad(v.reshape(b_, nc, CHUNK, N_HEADS, HEAD_DIM), pad)
    idx = jnp.arange(nc)[:, None] + jnp.arange(ATT_PAST_CHUNKS + 1)[None, :]
    kb = kc[:, idx].reshape(b_, nc, ATT_BAND, N_HEADS, HEAD_DIM)
    vb = vc[:, idx].reshape(b_, nc, ATT_BAND, N_HEADS, HEAD_DIM)
    rel = jnp.arange(CHUNK)[:, None] + ATT_PAST - jnp.arange(ATT_BAND)[None, :]
    kpos = jnp.arange(nc)[:, None] * CHUNK - ATT_PAST + jnp.arange(ATT_BAND)[None, :]
    valid = (kpos >= 0)[None, :, None, None, :]
    o = _band_attention(qc, kb, vb, rel, valid, rel_bias)
    return o.reshape(b_, L, ATT_DIM)


def _attn_sample(q, k, v, k_cache, v_cache, rel_bias):
    b_, t = q.shape[:2]
    lc = k_cache.shape[1]
    kk = jnp.concatenate([k_cache.astype(k.dtype), k], axis=1)
    vv = jnp.concatenate([v_cache.astype(v.dtype), v], axis=1)
    rel = jnp.arange(t)[:, None] + lc - jnp.arange(lc + t)[None, :]
    o = _band_attention(q, kk, vv, rel, True, rel_bias)
    return o.reshape(b_, t, ATT_DIM)


def _layer(x, ssm_conv_prefix, ssm_h0, sc_prefix, kv_cache, lp, ssm_block):
    b_, L = x.shape[:2]
    xn = _rmsnorm(x, lp['norm_mix_g'])
    proj = xn @ lp['w_in']
    z, xbc, dt_raw, sb, sc, sh, q, k, v, gates = jnp.split(proj, IN_OFFSETS, axis=-1)
    u_ssm, h_last, ssm_conv_state = _ssd_mixer(z, xbc, dt_raw, ssm_conv_prefix, ssm_h0, lp, ssm_block)
    u_sc, sc_state = _short_conv_mixer(sb, sc, sh, sc_prefix, lp)
    q = _rmsnorm(q.reshape(b_, L, N_HEADS, HEAD_DIM), lp['q_norm_g'])
    k = _rmsnorm(k.reshape(b_, L, N_HEADS, HEAD_DIM), lp['k_norm_g'])
    v = v.reshape(b_, L, N_HEADS, HEAD_DIM)
    if kv_cache is None:
        o = _attn_prompt(q, k, v, lp['rel_bias'])
        keep = min(ATT_PAST, L)
        k_rows, v_rows = k[:, L - keep:], v[:, L - keep:]
    else:
        o = _attn_sample(q, k, v, kv_cache[0], kv_cache[1], lp['rel_bias'])
        k_rows, v_rows = k, v
    u_attn = o @ lp['attn_out_w']
    g = jax.nn.sigmoid(gates.astype(jnp.float32)).reshape(b_, L, N_BRANCH, D_MODEL)
    u = jnp.stack([u_ssm, u_sc, u_attn], axis=2).astype(jnp.float32)
    merged = jnp.sum(g * u, axis=2).astype(x.dtype)
    h = x + merged @ lp['w_o']
    hn = _rmsnorm(h, lp['norm_ffn_g'])
    y = h + jnp.square(jax.nn.relu(hn @ lp['ffn_w1'])) @ lp['ffn_w2']
    return y, (k_rows, v_rows, h_last, ssm_conv_state, sc_state)


def setup_inputs(seed: int = 0) -> dict:
    key = jax.random.key(seed)
    ks = list(jax.random.split(key, 32))
    f32 = jnp.float32

    def nrm(i, shape, scale):
        return jax.random.normal(ks[i], shape, f32) * scale

    att_rows = min(ATT_PAST, PAST_LEN)
    dt0 = jnp.exp(jax.random.uniform(ks[10], (DEPTH, SSM_HEADS), f32, math.log(DT_MIN), math.log(DT_MAX)))
    dt_bias = dt0 + jnp.log(-jnp.expm1(-dt0))
    a_log = jnp.log(jax.random.uniform(ks[11], (DEPTH, SSM_HEADS), f32, 1.0, 16.0))
    return {
        'x_prompt': nrm(0, (BATCH, SEQ, D_MODEL), 1.0),
        'x_sample': nrm(1, (DEC_BATCH, DEC_SEQ, D_MODEL), 1.0),
        'cache_attn_k': nrm(2, (DEPTH, DEC_BATCH, att_rows, N_HEADS, HEAD_DIM), 1.0),
        'cache_attn_v': nrm(3, (DEPTH, DEC_BATCH, att_rows, N_HEADS, HEAD_DIM), 1.0),
        'state_ssm': nrm(4, (DEPTH, DEC_BATCH, SSM_HEADS, SSM_HEAD_DIM, SSM_STATE), 0.5),
        'state_ssm_conv': nrm(5, (DEPTH, DEC_BATCH, SSM_CONV - 1, SSM_CONV_DIM), 1.0),
        'state_short_conv': nrm(6, (DEPTH, DEC_BATCH, SC_WIDTH - 1, SC_DIM), 1.0),
        'norm_mix_g': 1.0 + nrm(7, (DEPTH, D_MODEL), 0.01),
        'w_in': nrm(8, (DEPTH, D_MODEL, IN_COLS), D_MODEL ** -0.5),
        'ssm_conv_w': nrm(9, (DEPTH, SSM_CONV, SSM_CONV_DIM), SSM_CONV ** -0.5),
        'ssm_conv_b': nrm(12, (DEPTH, SSM_CONV_DIM), 0.01),
        'ssm_dt_bias': dt_bias,
        'ssm_a_log': a_log,
        'ssm_d': 1.0 + nrm(13, (DEPTH, SSM_HEADS), 0.1),
        'ssm_norm_g': 1.0 + nrm(14, (DEPTH, SSM_INNER), 0.01),
        'ssm_out_w': nrm(15, (DEPTH, SSM_INNER, D_MODEL), SSM_INNER ** -0.5),
        'sc_conv_w': nrm(16, (DEPTH, SC_WIDTH, SC_DIM), SC_WIDTH ** -0.5),
        'sc_out_w': nrm(17, (DEPTH, SC_DIM, D_MODEL), SC_DIM ** -0.5),
        'q_norm_g': 1.0 + nrm(18, (DEPTH, HEAD_DIM), 0.01),
        'k_norm_g': 1.0 + nrm(19, (DEPTH, HEAD_DIM), 0.01),
        'rel_bias': nrm(20, (DEPTH, N_HEADS, N_REL), 0.1),
        'attn_out_w': nrm(21, (DEPTH, ATT_DIM, D_MODEL), ATT_DIM ** -0.5),
        'w_o': nrm(22, (DEPTH, D_MODEL, D_MODEL), D_MODEL ** -0.5),
        'norm_ffn_g': 1.0 + nrm(23, (DEPTH, D_MODEL), 0.01),
        'ffn_w1': nrm(24, (DEPTH, D_MODEL, D_FF), D_MODEL ** -0.5),
        'ffn_w2': nrm(25, (DEPTH, D_FF, D_MODEL), D_FF ** -0.5),
    }


def reference(x_prompt, x_sample, cache_attn_k, cache_attn_v, state_ssm, state_ssm_conv, state_short_conv,
              norm_mix_g, w_in, ssm_conv_w, ssm_conv_b, ssm_dt_bias, ssm_a_log, ssm_d, ssm_norm_g, ssm_out_w,
              sc_conv_w, sc_out_w, q_norm_g, k_norm_g, rel_bias, attn_out_w, w_o, norm_ffn_g, ffn_w1, ffn_w2):
    yp, ys = x_prompt, x_sample
    bp = x_prompt.shape[0]
    new_p, new_s = [], []
    for l in range(DEPTH):
        lp = {
            'norm_mix_g': norm_mix_g[l], 'w_in': w_in[l],
            'ssm_conv_w': ssm_conv_w[l], 'ssm_conv_b': ssm_conv_b[l], 'ssm_dt_bias': ssm_dt_bias[l],
            'ssm_a_log': ssm_a_log[l], 'ssm_d': ssm_d[l], 'ssm_norm_g': ssm_norm_g[l], 'ssm_out_w': ssm_out_w[l],
            'sc_conv_w': sc_conv_w[l], 'sc_out_w': sc_out_w[l],
            'q_norm_g': q_norm_g[l], 'k_norm_g': k_norm_g[l], 'rel_bias': rel_bias[l], 'attn_out_w': attn_out_w[l],
            'w_o': w_o[l], 'norm_ffn_g': norm_ffn_g[l], 'ffn_w1': ffn_w1[l], 'ffn_w2': ffn_w2[l],
        }
        zc = jnp.zeros((bp, SSM_CONV - 1, SSM_CONV_DIM), x_prompt.dtype)
        zh = jnp.zeros((bp, SSM_HEADS, SSM_HEAD_DIM, SSM_STATE), jnp.float32)
        zs = jnp.zeros((bp, SC_WIDTH - 1, SC_DIM), x_prompt.dtype)
        yp, st_p = _layer(yp, zc, zh, zs, None, lp, SSM_BLOCK)
        ys, st_s = _layer(ys, state_ssm_conv[l], state_ssm[l], state_short_conv[l],
                          (cache_attn_k[l], cache_attn_v[l]), lp, x_sample.shape[1])
        new_p.append(st_p)
        new_s.append(st_s)
    k_p = jnp.stack([s[0] for s in new_p])
    v_p = jnp.stack([s[1] for s in new_p])
    k_s = jnp.stack([s[0] for s in new_s])
    v_s = jnp.stack([s[1] for s in new_s])
    ssm_p = jnp.stack([s[2] for s in new_p])
    ssm_s = jnp.stack([s[2] for s in new_s])
    ssmc_p = jnp.stack([s[3] for s in new_p])
    ssmc_s = jnp.stack([s[3] for s in new_s])
    scc_p = jnp.stack([s[4] for s in new_p])
    scc_s = jnp.stack([s[4] for s in new_s])
    return (yp, ys, k_p, v_p, k_s, v_s, ssm_p, ssm_s, ssmc_p, ssmc_s, scc_p, scc_s)
```

```python
import functools
import math

import jax
import jax.numpy as jnp
from jax import lax
from jax.experimental import pallas as pl
from jax.experimental.pallas import tpu as pltpu

F32 = jnp.float32
BF16 = jnp.bfloat16

D_MODEL = 2048
DEPTH = 2
CHUNK = 64
EPS = 1e-6

SSM_INNER = D_MODEL
SSM_HEAD_DIM = 64
SSM_HEADS = SSM_INNER // SSM_HEAD_DIM
SSM_GROUPS = 4
SSM_STATE = 128
SSM_CONV = 4
SSM_BC = SSM_GROUPS * SSM_STATE
SSM_CONV_DIM = SSM_INNER + 2 * SSM_BC
SSM_GROUP_DIM = SSM_INNER // SSM_GROUPS

SC_DIM = D_MODEL
SC_WIDTH = 3

N_HEADS = 16
HEAD_DIM = D_MODEL // N_HEADS
ATT_DIM = N_HEADS * HEAD_DIM
ATT_PAST_CHUNKS = 8
ATT_PAST = ATT_PAST_CHUNKS * CHUNK
ATT_BAND = (ATT_PAST_CHUNKS + 1) * CHUNK
MAX_REL = 128

N_BRANCH = 3
D_FF = 4 * D_MODEL

IN_SPLITS = (SSM_INNER, SSM_CONV_DIM, SSM_HEADS, SC_DIM, SC_DIM, SC_DIM, ATT_DIM, ATT_DIM, ATT_DIM,
             N_BRANCH * D_MODEL)
IN_OFFSETS = tuple(int(sum(IN_SPLITS[:i])) for i in range(len(IN_SPLITS) + 1))

NEG_INF = -1e30

LANES = 128
SUBLANES = 8
MIB = 1024 * 1024

ATT_Q_CHUNKS = 4
ATT_Q_BLOCK = ATT_Q_CHUNKS * CHUNK
ATT_K_BLOCKS = (ATT_PAST_CHUNKS + ATT_Q_CHUNKS) // ATT_Q_CHUNKS
ATT_WINDOW = ATT_K_BLOCKS * ATT_Q_BLOCK


def _cparams(semantics, vmem_mib):
    return pltpu.CompilerParams(dimension_semantics=semantics, vmem_limit_bytes=vmem_mib * MIB)


def _dot(a, b):
    return jnp.dot(a, b, preferred_element_type=F32)


def _dot_nt(a, b):
    return lax.dot_general(a, b, (((1,), (1,)), ((), ())), preferred_element_type=F32)


def _split_bf16(v, parts):
    out = []
    r = v
    for _ in range(parts):
        p = r.astype(BF16)
        out.append(p)
        r = r - p.astype(F32)
    return out


def _dot_exact_lhs(v, m, parts):
    acc = None
    for p in _split_bf16(v, parts):
        t = _dot(p, m)
        acc = t if acc is None else acc + t
    return acc


def _dot_exact_rhs(m, v, parts):
    acc = None
    for p in _split_bf16(v, parts):
        t = _dot(m, p)
        acc = t if acc is None else acc + t
    return acc


def _rmsnorm_kernel(x_ref, g_ref, o_ref):
    x = x_ref[...]
    ms = jnp.mean(x * x, axis=-1, keepdims=True)
    o_ref[...] = (x * lax.rsqrt(ms + EPS) * g_ref[...]).astype(o_ref.dtype)


def _rmsnorm(x, g, tm):
    t, d = x.shape
    return pl.pallas_call(
        _rmsnorm_kernel,
        grid=(t // tm,),
        in_specs=[pl.BlockSpec((tm, d), lambda i: (i, 0)), pl.BlockSpec((1, d), lambda i: (0, 0))],
        out_specs=pl.BlockSpec((tm, d), lambda i: (i, 0)),
        out_shape=jax.ShapeDtypeStruct((t, d), BF16),
        compiler_params=_cparams(("arbitrary",), 40),
        name="rmsnorm",
    )(x, g.reshape(1, d))


def _proj_act_kernel(*refs, act, has_bias, n_out):
    x_ref, w_ref = refs[0], refs[1]
    pos = 2
    acc = _dot(x_ref[...], w_ref[...])
    if has_bias:
        acc = acc + refs[pos][...]
        pos += 1
    y = act(acc)
    for o_ref in refs[pos:pos + n_out]:
        o_ref[...] = y.astype(o_ref.dtype)


def _proj_act(xn, w, *, tm, tn, act, out_dtypes, bias=None, name):
    t, k = xn.shape
    n = w.shape[1]
    in_specs = [pl.BlockSpec((tm, k), lambda i, j: (i, 0)), pl.BlockSpec((k, tn), lambda i, j: (0, j))]
    args = [xn, w]
    if bias is not None:
        in_specs.append(pl.BlockSpec((1, tn), lambda i, j: (0, j)))
        args.append(bias.reshape(1, n))
    outs = pl.pallas_call(
        functools.partial(_proj_act_kernel, act=act, has_bias=bias is not None, n_out=len(out_dtypes)),
        grid=(t // tm, n // tn),
        in_specs=in_specs,
        out_specs=[pl.BlockSpec((tm, tn), lambda i, j: (i, j)) for _ in out_dtypes],
        out_shape=[jax.ShapeDtypeStruct((t, n), dt) for dt in out_dtypes],
        compiler_params=_cparams(("arbitrary", "arbitrary"), 48),
        name=name,
    )(*args)
    return outs


def _proj_headnorm_kernel(x_ref, w_ref, g_ref, *o_refs, scale):
    acc = _dot(x_ref[...], w_ref[...])
    g = g_ref[...]
    for h in range(acc.shape[1] // HEAD_DIM):
        sl = slice(h * HEAD_DIM, (h + 1) * HEAD_DIM)
        blk = acc[:, sl]
        ms = jnp.mean(blk * blk, axis=-1, keepdims=True)
        y = blk * lax.rsqrt(ms + EPS) * g
        for o_ref in o_refs:
            if o_ref.dtype == BF16:
                o_ref[:, sl] = (y * scale).astype(BF16)
            else:
                o_ref[:, sl] = y


def _proj_headnorm(xn, w, g, *, tm, tn, scale, out_dtypes, name):
    t, k = xn.shape
    n = w.shape[1]
    return pl.pallas_call(
        functools.partial(_proj_headnorm_kernel, scale=scale),
        grid=(t // tm, n // tn),
        in_specs=[pl.BlockSpec((tm, k), lambda i, j: (i, 0)), pl.BlockSpec((k, tn), lambda i, j: (0, j)),
                  pl.BlockSpec((1, HEAD_DIM), lambda i, j: (0, 0))],
        out_specs=[pl.BlockSpec((tm, tn), lambda i, j: (i, j)) for _ in out_dtypes],
        out_shape=[jax.ShapeDtypeStruct((t, n), dt) for dt in out_dtypes],
        compiler_params=_cparams(("arbitrary", "arbitrary"), 48),
        name=name,
    )(xn, w, g.reshape(1, HEAD_DIM))


def _causal_conv(u, cw_ref, p_ref, carry_ref, st_ref, i, j, *, width, nseg, tiles_per_seq):
    tm, tn = u.shape
    seg_len = tm // nseg
    row8 = lax.broadcasted_iota(jnp.int32, (SUBLANES, tn), 0)
    if tiles_per_seq > 1:
        @pl.when(lax.rem(i, tiles_per_seq) == 0)
        def _():
            carry_ref[j] = p_ref[0]
    outs = []
    for s in range(nseg):
        seg = u[s * seg_len:(s + 1) * seg_len]
        prev8 = carry_ref[j] if tiles_per_seq > 1 else p_ref[s]
        acc = cw_ref[width - 1:width, :] * seg
        for k in range(1, width):
            sh = pltpu.roll(seg, k, 0)
            first8 = jnp.where(row8 < k, pltpu.roll(prev8, k, 0), sh[0:SUBLANES])
            shk = jnp.concatenate([first8, sh[SUBLANES:]], axis=0)
            acc = acc + cw_ref[width - 1 - k:width - k, :] * shk
        outs.append(acc)
        st_ref[s] = seg[seg_len - SUBLANES:seg_len]
    if tiles_per_seq > 1:
        carry_ref[j] = u[tm - SUBLANES:tm]
    return outs[0] if nseg == 1 else jnp.concatenate(outs, axis=0)


def _proj_xbc_kernel(x_ref, w_ref, cw_ref, cb_ref, p_ref, o_ref, st_ref, carry_ref, *, nseg, tiles_per_seq):
    i, j = pl.program_id(0), pl.program_id(1)
    u = _dot(x_ref[...], w_ref[...])
    y = _causal_conv(u, cw_ref, p_ref, carry_ref, st_ref, i, j, width=SSM_CONV, nseg=nseg,
                     tiles_per_seq=tiles_per_seq)
    y = y + cb_ref[...]
    o_ref[...] = y * jax.nn.sigmoid(y)


def _proj_sc_kernel(x_ref, wb_ref, wc_ref, wh_ref, cw_ref, p_ref, o_ref, st_ref, carry_ref, *, nseg,
                    tiles_per_seq):
    i, j = pl.program_id(0), pl.program_id(1)
    x = x_ref[...]
    u = _dot(x, wc_ref[...]) * _dot(x, wh_ref[...])
    v = _causal_conv(u, cw_ref, p_ref, carry_ref, st_ref, i, j, width=SC_WIDTH, nseg=nseg,
                     tiles_per_seq=tiles_per_seq)
    o_ref[...] = (_dot(x, wb_ref[...]) * v).astype(o_ref.dtype)


def _proj_conv(kernel, xn, ws, conv_w8, conv_b, prefix8, *, tm, tn, seq_len, out_dtype, name):
    t, k = xn.shape
    n = ws[0].shape[1]
    nseq = prefix8.shape[0]
    nseg = max(1, tm // seq_len)
    tiles_per_seq = max(1, seq_len // tm)
    ncol = n // tn
    if tiles_per_seq > 1:
        seq_map = lambda i, j: (i // tiles_per_seq, 0, j)
    else:
        seq_map = lambda i, j: (i, 0, j)
    in_specs = [pl.BlockSpec((tm, k), lambda i, j: (i, 0))]
    in_specs += [pl.BlockSpec((k, tn), lambda i, j: (0, j)) for _ in ws]
    in_specs.append(pl.BlockSpec((SUBLANES, tn), lambda i, j: (0, j)))
    args = [xn, *ws, conv_w8]
    if conv_b is not None:
        in_specs.append(pl.BlockSpec((1, tn), lambda i, j: (0, j)))
        args.append(conv_b.reshape(1, n))
    in_specs.append(pl.BlockSpec((nseg, SUBLANES, tn), seq_map))
    args.append(prefix8)
    y, tails = pl.pallas_call(
        functools.partial(kernel, nseg=nseg, tiles_per_seq=tiles_per_seq),
        grid=(t // tm, ncol),
        in_specs=in_specs,
        out_specs=[pl.BlockSpec((tm, tn), lambda i, j: (i, j)),
                   pl.BlockSpec((nseg, SUBLANES, tn), lambda i, j: (i, 0, j))],
        out_shape=[jax.ShapeDtypeStruct((t, n), out_dtype),
                   jax.ShapeDtypeStruct((t // tm * nseg, SUBLANES, n), F32)],
        scratch_shapes=[pltpu.VMEM((ncol, SUBLANES, tn), F32)],
        compiler_params=_cparams(("arbitrary", "arbitrary"), 48),
        name=name,
    )(*args)
    return y, tails[tiles_per_seq - 1::tiles_per_seq]


def _ssd_kernel(x_ref, b_ref, c_ref, dt_ref, zs_ref, alog_ref, dx_ref, ng_ref, ech_ref, els_ref, h0_ref,
                y_ref, hl_ref, ht_ref, *, L, nc):
    ci = pl.program_id(1)
    hp = LANES // L
    ntiles = SSM_HEADS // hp
    gw = SSM_GROUP_DIM

    @pl.when(ci == 0)
    def _():
        ht_ref[...] = h0_ref[0].T

    x = x_ref[...]
    bm = b_ref[...]
    cm = c_ref[...]
    dt = dt_ref[...]
    a = -jnp.exp(alog_ref[...])
    da = dt * a

    ri = lax.broadcasted_iota(jnp.int32, (L, L), 0)
    cj = lax.broadcasted_iota(jnp.int32, (L, L), 1)
    tri = (ri >= cj).astype(BF16)
    acum = _dot_exact_rhs(tri, da, 3)
    eacum = jnp.exp(acum)
    dend = jnp.exp(acum[L - 1:L, :] - acum)
    w = dt * dend

    ech = ech_ref[...]
    cexp = _dot_exact_lhs(acum, els_ref[...], 3)
    wx = _dot_exact_lhs(w, ech, 2)
    ex = _dot_exact_lhs(eacum, ech, 2)

    tile_rows = lambda v: jnp.concatenate([v] * hp, axis=0)
    acum_t = tile_rows(acum).T
    dt_t = tile_rows(dt).T

    lane = lax.broadcasted_iota(jnp.int32, (1, LANES), 1)
    log2_l = L.bit_length() - 1
    lane_blk = jnp.right_shift(lane, log2_l)
    row_l = lax.broadcasted_iota(jnp.int32, (L, LANES), 0)
    lane_s = jnp.bitwise_and(lax.broadcasted_iota(jnp.int32, (L, LANES), 1), L - 1)
    causal = row_l >= lane_s

    bsq = [tile_rows(bm[:, g * SSM_STATE:(g + 1) * SSM_STATE]) for g in range(SSM_GROUPS)]
    cbt = [_dot_nt(cm[:, g * SSM_STATE:(g + 1) * SSM_STATE].astype(BF16), bsq[g].astype(BF16))
           for g in range(SSM_GROUPS)]

    tw = hp * SSM_HEAD_DIM
    rb = jnp.right_shift(lax.broadcasted_iota(jnp.int32, (LANES, tw), 0), log2_l)
    cb_ = jnp.right_shift(lax.broadcasted_iota(jnp.int32, (LANES, tw), 1), SSM_HEAD_DIM.bit_length() - 1)
    blockdiag = rb == cb_

    yd = []
    for t in range(ntiles):
        h_first = t * hp
        g = h_first // (SSM_HEADS // SSM_GROUPS)
        r_row = acum_t[h_first:h_first + 1, :]
        d_row = dt_t[h_first:h_first + 1, :]
        for jj in range(1, hp):
            sel = lane_blk == jj
            r_row = jnp.where(sel, acum_t[h_first + jj:h_first + jj + 1, :], r_row)
            d_row = jnp.where(sel, dt_t[h_first + jj:h_first + jj + 1, :], d_row)
        diff = cexp[:, t * LANES:(t + 1) * LANES] - r_row
        dec = jnp.exp(jnp.where(causal, diff, -jnp.inf))
        sc = (cbt[g] * dec * d_row).astype(BF16)
        xs = tile_rows(x[:, t * tw:(t + 1) * tw])
        rhs = jnp.where(blockdiag, xs, 0.0).astype(BF16)
        yd.append(_dot(sc, rhs))
    y = jnp.concatenate(yd, axis=1)

    xw = x * wx
    zpad = jnp.zeros((LANES - L, gw), F32)
    for g in range(SSM_GROUPS):
        gs = slice(g * gw, (g + 1) * gw)
        h_in = ht_ref[:, gs]
        y_off = _dot(cm[:, g * SSM_STATE:(g + 1) * SSM_STATE].astype(BF16), h_in.astype(BF16))
        yg = y[:, gs] + y_off * ex[:, gs] + dx_ref[:, gs] * x[:, gs]
        yg = yg * zs_ref[:, gs]
        ms = jnp.mean(yg * yg, axis=-1, keepdims=True)
        y_ref[:, gs] = (yg * lax.rsqrt(ms + EPS) * ng_ref[:, gs]).astype(y_ref.dtype)
        bm_t = bsq[g].T.astype(BF16)
        upd = jnp.concatenate([xw[:, gs], zpad], axis=0).astype(BF16)
        ht_ref[:, gs] = h_in * ex[L - 1:L, gs] + _dot(bm_t, upd)

    @pl.when(ci == nc - 1)
    def _():
        hl_ref[0] = ht_ref[...].T


def _ssd(xbc_act, dt, zs, a_log128, d_x, norm_g, ech, els, h0, *, L, seq_len):
    t = xbc_act.shape[0]
    nseq = h0.shape[0]
    nc = seq_len // L
    rmap = lambda b, c: (b * nc + c, 0)
    cmap = lambda b, c: (0, 0)
    nb = SSM_INNER // SSM_BC
    return pl.pallas_call(
        functools.partial(_ssd_kernel, L=L, nc=nc),
        grid=(nseq, nc),
        in_specs=[pl.BlockSpec((L, SSM_INNER), rmap),
                  pl.BlockSpec((L, SSM_BC), lambda b, c: (b * nc + c, nb)),
                  pl.BlockSpec((L, SSM_BC), lambda b, c: (b * nc + c, nb + 1)),
                  pl.BlockSpec((L, LANES), rmap),
                  pl.BlockSpec((L, SSM_INNER), rmap),
                  pl.BlockSpec((1, LANES), cmap),
                  pl.BlockSpec((1, SSM_INNER), cmap),
                  pl.BlockSpec((1, SSM_INNER), cmap),
                  pl.BlockSpec(ech.shape, cmap),
                  pl.BlockSpec(els.shape, cmap),
                  pl.BlockSpec((1, SSM_INNER, SSM_STATE), lambda b, c: (b, 0, 0))],
        out_specs=[pl.BlockSpec((L, SSM_INNER), rmap),
                   pl.BlockSpec((1, SSM_INNER, SSM_STATE), lambda b, c: (b, 0, 0))],
        out_shape=[jax.ShapeDtypeStruct((t, SSM_INNER), BF16),
                   jax.ShapeDtypeStruct((nseq, SSM_INNER, SSM_STATE), F32)],
        scratch_shapes=[pltpu.VMEM((SSM_STATE, SSM_INNER), F32)],
        compiler_params=_cparams(("arbitrary", "arbitrary"), 48),
        name=f"ssd_L{L}",
    )(xbc_act, xbc_act, xbc_act, dt, zs, a_log128, d_x, norm_g, ech, els, h0)


def _softmax_pv(s_parts, v_parts):
    m = functools.reduce(jnp.maximum, [jnp.max(s, axis=1, keepdims=True) for s in s_parts])
    l = None
    o = None
    for s, v in zip(s_parts, v_parts):
        p = jnp.exp(s - m)
        ls = jnp.sum(p, axis=1, keepdims=True)
        os_ = _dot(p.astype(BF16), v)
        l = ls if l is None else l + ls
        o = os_ if o is None else o + os_
    return o / l


def _attn_prompt_kernel(q_ref, k0_ref, k1_ref, k2_ref, v0_ref, v1_ref, v2_ref, bias_ref, o_ref):
    i = pl.program_id(1)
    k_refs = (k0_ref, k1_ref, k2_ref)
    v_refs = (v0_ref, v1_ref, v2_ref)
    first_valid = (ATT_K_BLOCKS - 1 - i) * ATT_Q_BLOCK
    kidx = lax.broadcasted_iota(jnp.int32, (ATT_Q_BLOCK, ATT_Q_BLOCK), 1)
    for h in range(N_HEADS):
        sl = slice(h * HEAD_DIM, (h + 1) * HEAD_DIM)
        q = q_ref[:, sl]
        s_parts = []
        for kb in range(ATT_K_BLOCKS):
            s = _dot_nt(q, k_refs[kb][:, sl]) + bias_ref[h, :, kb * ATT_Q_BLOCK:(kb + 1) * ATT_Q_BLOCK]
            s_parts.append(jnp.where(kidx + kb * ATT_Q_BLOCK >= first_valid, s, NEG_INF))
        o = _softmax_pv(s_parts, [v_refs[kb][:, sl] for kb in range(ATT_K_BLOCKS)])
        o_ref[:, sl] = o.astype(o_ref.dtype)


def _attn_prompt(q, k, v, bias, *, nseq, seq_len):
    t = q.shape[0]
    nqb = seq_len // ATT_Q_BLOCK
    qmap = lambda b, i: (b * nqb + i, 0)

    def kmap(back):
        return lambda b, i: (b * nqb + jnp.maximum(i - back, 0), 0)

    blk = (ATT_Q_BLOCK, ATT_DIM)
    kv_specs = [pl.BlockSpec(blk, kmap(ATT_K_BLOCKS - 1 - kb)) for kb in range(ATT_K_BLOCKS)]
    return pl.pallas_call(
        _attn_prompt_kernel,
        grid=(nseq, nqb),
        in_specs=[pl.BlockSpec(blk, qmap)] + kv_specs + kv_specs
        + [pl.BlockSpec(bias.shape, lambda b, i: (0, 0, 0))],
        out_specs=pl.BlockSpec(blk, qmap),
        out_shape=jax.ShapeDtypeStruct((t, ATT_DIM), BF16),
        compiler_params=_cparams(("arbitrary", "arbitrary"), 56),
        name="attn_prompt",
    )(q, k, k, k, v, v, v, bias)


def _attn_sample_kernel(q_ref, kn_ref, vn_ref, kc_ref, vc_ref, bias_ref, o_ref):
    lc = kc_ref.shape[1]
    for h in range(N_HEADS):
        sl = slice(h * HEAD_DIM, (h + 1) * HEAD_DIM)
        q = q_ref[:, sl]
        s_c = _dot_nt(q, kc_ref[0, :, sl].astype(BF16)) + bias_ref[h, :, 0:lc]
        s_n = _dot_nt(q, kn_ref[:, sl]) + bias_ref[h, :, lc:]
        o = _softmax_pv([s_c, s_n], [vc_ref[0, :, sl].astype(BF16), vn_ref[:, sl]])
        o_ref[:, sl] = o.astype(o_ref.dtype)


def _attn_sample(q, k, v, k_cache, v_cache, bias, *, nseq, seq_len):
    t = q.shape[0]
    lc = k_cache.shape[1]
    blk = (seq_len, ATT_DIM)
    rmap = lambda b: (b, 0)
    cspec = pl.BlockSpec((1, lc, ATT_DIM), lambda b: (b, 0, 0))
    return pl.pallas_call(
        _attn_sample_kernel,
        grid=(nseq,),
        in_specs=[pl.BlockSpec(blk, rmap), pl.BlockSpec(blk, rmap), pl.BlockSpec(blk, rmap), cspec, cspec,
                  pl.BlockSpec(bias.shape, lambda b: (0, 0, 0))],
        out_specs=pl.BlockSpec(blk, rmap),
        out_shape=jax.ShapeDtypeStruct((t, ATT_DIM), BF16),
        compiler_params=_cparams(("arbitrary",), 48),
        name="attn_sample",
    )(q, k, v, k_cache, v_cache, bias)


def _merge_kernel(ys_ref, yc_ref, ya_ref, ws_ref, wc_ref, wa_ref, g0_ref, g1_ref, g2_ref, o_ref):
    m = g0_ref[...] * _dot(ys_ref[...], ws_ref[...])
    m = m + g1_ref[...] * _dot(yc_ref[...], wc_ref[...])
    m = m + g2_ref[...] * _dot(ya_ref[...], wa_ref[...])
    o_ref[...] = m.astype(o_ref.dtype)


def _merge(y_ssm, y_sc, y_att, w_ssm, w_sc, w_att, gates, *, tm, tn):
    t, k = y_ssm.shape
    n = w_ssm.shape[1]
    ncol = n // tn
    lhs = pl.BlockSpec((tm, k), lambda i, j: (i, 0))
    rhs = pl.BlockSpec((k, tn), lambda i, j: (0, j))

    def gspec(b):
        return pl.BlockSpec((tm, tn), lambda i, j: (i, b * ncol + j))

    return pl.pallas_call(
        _merge_kernel,
        grid=(t // tm, ncol),
        in_specs=[lhs, lhs, lhs, rhs, rhs, rhs, gspec(0), gspec(1), gspec(2)],
        out_specs=pl.BlockSpec((tm, tn), lambda i, j: (i, j)),
        out_shape=jax.ShapeDtypeStruct((t, n), BF16),
        compiler_params=_cparams(("arbitrary", "arbitrary"), 48),
        name="merge",
    )(y_ssm, y_sc, y_att, w_ssm, w_sc, w_att, gates, gates, gates)


def _wo_kernel(x_ref, m_ref, w_ref, g_ref, h_ref, hn_ref):
    h = x_ref[...] + _dot(m_ref[...], w_ref[...])
    h_ref[...] = h
    ms = jnp.mean(h * h, axis=-1, keepdims=True)
    hn_ref[...] = (h * lax.rsqrt(ms + EPS) * g_ref[...]).astype(hn_ref.dtype)


def _wo(x, merged, w_o, g, *, tm):
    t, d = x.shape
    row = lambda i: (i, 0)
    const = lambda i: (0, 0)
    return pl.pallas_call(
        _wo_kernel,
        grid=(t // tm,),
        in_specs=[pl.BlockSpec((tm, d), row), pl.BlockSpec((tm, d), row), pl.BlockSpec((d, d), const),
                  pl.BlockSpec((1, d), const)],
        out_specs=[pl.BlockSpec((tm, d), row), pl.BlockSpec((tm, d), row)],
        out_shape=[jax.ShapeDtypeStruct((t, d), F32), jax.ShapeDtypeStruct((t, d), BF16)],
        compiler_params=_cparams(("arbitrary",), 52),
        name="wo",
    )(x, merged, w_o, g.reshape(1, d))


def _ffn_kernel(h_ref, hn_ref, w1_ref, w2_ref, o_ref):
    c = pl.program_id(1)
    a = _dot(hn_ref[...], w1_ref[...])
    a = jnp.square(jnp.maximum(a, 0.0)).astype(BF16)
    contrib = _dot(a, w2_ref[...])

    @pl.when(c == 0)
    def _():
        o_ref[...] = h_ref[...] + contrib

    @pl.when(c > 0)
    def _():
        o_ref[...] += contrib


def _ffn(h, hn, w1, w2, *, tm, tc):
    t, d = h.shape
    dff = w1.shape[1]
    row = lambda i, c: (i, 0)
    return pl.pallas_call(
        _ffn_kernel,
        grid=(t // tm, dff // tc),
        in_specs=[pl.BlockSpec((tm, d), row), pl.BlockSpec((tm, d), row),
                  pl.BlockSpec((d, tc), lambda i, c: (0, c)), pl.BlockSpec((tc, d), lambda i, c: (c, 0))],
        out_specs=pl.BlockSpec((tm, d), row),
        out_shape=jax.ShapeDtypeStruct((t, d), F32),
        compiler_params=_cparams(("arbitrary", "arbitrary"), 52),
        name="ffn",
    )(h, hn, w1, w2)


def _head_expand(lanes_per_head):
    rows = lax.broadcasted_iota(jnp.int32, (LANES, SSM_HEADS * lanes_per_head), 0)
    cols = lax.broadcasted_iota(jnp.int32, (LANES, SSM_HEADS * lanes_per_head), 1) // lanes_per_head
    return (rows == cols).astype(BF16)


def _rel_bias_table(rel_bias, rel):
    return rel_bias.astype(F32)[:, jnp.clip(rel, -MAX_REL, MAX_REL) + MAX_REL]


def _prompt_bias(rel_bias):
    qi = jnp.arange(ATT_Q_BLOCK)[:, None]
    kj = jnp.arange(ATT_WINDOW)[None, :]
    band = kj - (qi // CHUNK) * CHUNK
    rel = (qi % CHUNK) + ATT_PAST - band
    in_band = (band >= 0) & (band < ATT_BAND)
    return jnp.where(in_band[None], _rel_bias_table(rel_bias, rel), NEG_INF)


def _sample_bias(rel_bias, t, lc):
    rel = jnp.arange(t)[:, None] + lc - jnp.arange(lc + t)[None, :]
    return _rel_bias_table(rel_bias, rel)


def _pad_rows_to8(a, axis):
    pad = [(0, 0)] * a.ndim
    pad[axis] = (SUBLANES - a.shape[axis], 0)
    return jnp.pad(a, pad)


def _layer(x, lw, *, nseq, seq_len, tm, ssd_chunk, ssm_conv_prefix, ssm_h0, sc_prefix, kv_cache):
    xn = _rmsnorm(x, lw["norm_mix_g"], tm)
    tn = 1024 if tm >= 1024 else 2048

    (zs,) = _proj_act(xn, lw["w_z"], tm=tm, tn=min(tn, SSM_INNER), act=lambda a: a * jax.nn.sigmoid(a),
                      out_dtypes=[F32], name="proj_z")
    (dt,) = _proj_act(xn, lw["w_dt"], tm=tm, tn=LANES, act=jax.nn.softplus, out_dtypes=[F32],
                      bias=lw["dt_bias128"], name="proj_dt")
    (gates,) = _proj_act(xn, lw["w_g"], tm=tm, tn=tn, act=jax.nn.sigmoid, out_dtypes=[F32], name="proj_gates")
    v32, v16 = _proj_act(xn, lw["w_v"], tm=tm, tn=tn, act=lambda a: a, out_dtypes=[F32, BF16], name="proj_v")
    (q16,) = _proj_headnorm(xn, lw["w_q"], lw["q_norm_g"], tm=tm, tn=tn, scale=HEAD_DIM ** -0.5,
                            out_dtypes=[BF16], name="proj_q")
    k32, k16 = _proj_headnorm(xn, lw["w_k"], lw["k_norm_g"], tm=tm, tn=tn, scale=1.0,
                              out_dtypes=[F32, BF16], name="proj_k")
    xbc_act, ssm_conv_state = _proj_conv(_proj_xbc_kernel, xn, [lw["w_xbc"]], lw["ssm_conv_w8"],
                                         lw["ssm_conv_b"], ssm_conv_prefix, tm=tm, tn=512, seq_len=seq_len,
                                         out_dtype=F32, name="proj_xbc")
    y_sc, sc_state = _proj_conv(_proj_sc_kernel, xn, [lw["w_sb"], lw["w_sc"], lw["w_sh"]], lw["sc_conv_w8"],
                                None, sc_prefix, tm=tm, tn=512, seq_len=seq_len, out_dtype=BF16,
                                name="proj_sc")

    els = lw["ech"] if ssd_chunk == SSM_HEAD_DIM else _head_expand(ssd_chunk)
    y_ssm, h_last = _ssd(xbc_act, dt, zs, lw["a_log128"], lw["d_x"], lw["ssm_norm_g"], lw["ech"], els, ssm_h0,
                         L=ssd_chunk, seq_len=seq_len)

    if kv_cache is None:
        o = _attn_prompt(q16, k16, v16, lw["prompt_bias"], nseq=nseq, seq_len=seq_len)
    else:
        o = _attn_sample(q16, k16, v16, kv_cache[0], kv_cache[1], lw["sample_bias"], nseq=nseq,
                         seq_len=seq_len)

    mtm = min(tm, 512)
    merged = _merge(y_ssm, y_sc, o, lw["ssm_out_w"], lw["sc_out_w"], lw["attn_out_w"], gates, tm=mtm, tn=512)
    h, hn = _wo(x, merged, lw["w_o"], lw["norm_ffn_g"], tm=mtm)
    y = _ffn(h, hn, lw["ffn_w1"], lw["ffn_w2"], tm=mtm, tc=1024)
    return y, (k32, v32, h_last, ssm_conv_state, sc_state)


def kernel(x_prompt, x_sample, cache_attn_k, cache_attn_v, state_ssm, state_ssm_conv, state_short_conv,
           norm_mix_g, w_in, ssm_conv_w, ssm_conv_b, ssm_dt_bias, ssm_a_log, ssm_d, ssm_norm_g, ssm_out_w,
           sc_conv_w, sc_out_w, q_norm_g, k_norm_g, rel_bias, attn_out_w, w_o, norm_ffn_g, ffn_w1, ffn_w2):
    bp, lp, d = x_prompt.shape
    bs, ls, _ = x_sample.shape
    lc = cache_attn_k.shape[2]
    ech = _head_expand(SSM_HEAD_DIM)

    yp = x_prompt.reshape(bp * lp, d)
    ys = x_sample.reshape(bs * ls, d)
    new_p, new_s = [], []
    for l in range(DEPTH):
        seg = lambda i: w_in[l][:, IN_OFFSETS[i]:IN_OFFSETS[i + 1]].astype(BF16)
        lw = {
            "norm_mix_g": norm_mix_g[l], "norm_ffn_g": norm_ffn_g[l],
            "w_z": seg(0), "w_xbc": seg(1),
            "w_dt": jnp.pad(seg(2), ((0, 0), (0, LANES - SSM_HEADS))),
            "w_sb": seg(3), "w_sc": seg(4), "w_sh": seg(5), "w_q": seg(6), "w_k": seg(7), "w_v": seg(8),
            "w_g": seg(9),
            "dt_bias128": jnp.pad(ssm_dt_bias[l].astype(F32), (0, LANES - SSM_HEADS)),
            "a_log128": jnp.pad(ssm_a_log[l].astype(F32), (0, LANES - SSM_HEADS)).reshape(1, LANES),
            "d_x": jnp.repeat(ssm_d[l].astype(F32), SSM_HEAD_DIM).reshape(1, SSM_INNER),
            "ssm_norm_g": ssm_norm_g[l].astype(F32).reshape(1, SSM_INNER),
            "ssm_conv_w8": jnp.pad(ssm_conv_w[l].astype(F32), ((0, SUBLANES - SSM_CONV), (0, 0))),
            "ssm_conv_b": ssm_conv_b[l].astype(F32),
            "sc_conv_w8": jnp.pad(sc_conv_w[l].astype(F32), ((0, SUBLANES - SC_WIDTH), (0, 0))),
            "q_norm_g": q_norm_g[l].astype(F32), "k_norm_g": k_norm_g[l].astype(F32),
            "ssm_out_w": ssm_out_w[l].astype(BF16), "sc_out_w": sc_out_w[l].astype(BF16),
            "attn_out_w": attn_out_w[l].astype(BF16), "w_o": w_o[l].astype(BF16),
            "ffn_w1": ffn_w1[l].astype(BF16), "ffn_w2": ffn_w2[l].astype(BF16),
            "ech": ech,
            "prompt_bias": _prompt_bias(rel_bias[l]),
            "sample_bias": _sample_bias(rel_bias[l], ls, lc),
        }
        yp, st_p = _layer(
            yp, lw, nseq=bp, seq_len=lp, tm=1024, ssd_chunk=CHUNK,
            ssm_conv_prefix=jnp.zeros((bp, SUBLANES, SSM_CONV_DIM), F32),
            ssm_h0=jnp.zeros((bp, SSM_INNER, SSM_STATE), F32),
            sc_prefix=jnp.zeros((bp, SUBLANES, SC_DIM), F32), kv_cache=None)
        ys, st_s = _layer(
            ys, lw, nseq=bs, seq_len=ls, tm=bs * ls, ssd_chunk=ls,
            ssm_conv_prefix=_pad_rows_to8(state_ssm_conv[l].astype(F32), 1),
            ssm_h0=state_ssm[l].astype(F32).reshape(bs, SSM_INNER, SSM_STATE),
            sc_prefix=_pad_rows_to8(state_short_conv[l].astype(F32), 1),
            kv_cache=(cache_attn_k[l].reshape(bs, lc, ATT_DIM), cache_attn_v[l].reshape(bs, lc, ATT_DIM)))
        new_p.append(st_p)
        new_s.append(st_s)

    keep = min(ATT_PAST, lp)

    def kv_rows(a, b, length, rows):
        return a.reshape(b, length, N_HEADS, HEAD_DIM)[:, length - rows:]

    def stack(states, fn):
        return jnp.stack([fn(s) for s in states])

    hshape = lambda b: (b, SSM_HEADS, SSM_HEAD_DIM, SSM_STATE)
    return (
        yp.reshape(bp, lp, d),
        ys.reshape(bs, ls, d),
        stack(new_p, lambda s: kv_rows(s[0], bp, lp, keep)),
        stack(new_p, lambda s: kv_rows(s[1], bp, lp, keep)),
        stack(new_s, lambda s: kv_rows(s[0], bs, ls, ls)),
        stack(new_s, lambda s: kv_rows(s[1], bs, ls, ls)),
        stack(new_p, lambda s: s[2].reshape(hshape(bp))),
        stack(new_s, lambda s: s[2].reshape(hshape(bs))),
        stack(new_p, lambda s: s[3][:, SUBLANES - (SSM_CONV - 1):]),
        stack(new_s, lambda s: s[3][:, SUBLANES - (SSM_CONV - 1):]),
        stack(new_p, lambda s: s[4][:, SUBLANES - (SC_WIDTH - 1):]),
        stack(new_s, lambda s: s[4][:, SUBLANES - (SC_WIDTH - 1):]),
    )
```

```python
import functools
import math

import jax
import jax.numpy as jnp
from jax import lax
from jax.experimental import pallas as pl
from jax.experimental.pallas import tpu as pltpu

F32 = jnp.float32
BF16 = jnp.bfloat16

D_MODEL = 2048
DEPTH = 2
CHUNK = 64
EPS = 1e-6

SSM_INNER = D_MODEL
SSM_HEAD_DIM = 64
SSM_HEADS = SSM_INNER // SSM_HEAD_DIM
SSM_GROUPS = 4
SSM_STATE = 128
SSM_CONV = 4
SSM_BC = SSM_GROUPS * SSM_STATE
SSM_CONV_DIM = SSM_INNER + 2 * SSM_BC
SSM_GROUP_DIM = SSM_INNER // SSM_GROUPS

SC_DIM = D_MODEL
SC_WIDTH = 3

N_HEADS = 16
HEAD_DIM = D_MODEL // N_HEADS
ATT_DIM = N_HEADS * HEAD_DIM
ATT_PAST_CHUNKS = 8
ATT_PAST = ATT_PAST_CHUNKS * CHUNK
ATT_BAND = (ATT_PAST_CHUNKS + 1) * CHUNK
MAX_REL = 128

N_BRANCH = 3
D_FF = 4 * D_MODEL

IN_SPLITS = (SSM_INNER, SSM_CONV_DIM, SSM_HEADS, SC_DIM, SC_DIM, SC_DIM, ATT_DIM, ATT_DIM, ATT_DIM,
             N_BRANCH * D_MODEL)
IN_OFFSETS = tuple(int(sum(IN_SPLITS[:i])) for i in range(len(IN_SPLITS) + 1))

NEG_INF = -1e30

LANES = 128
SUBLANES = 8
MIB = 1024 * 1024

ATT_Q_CHUNKS = 4
ATT_Q_BLOCK = ATT_Q_CHUNKS * CHUNK
ATT_K_BLOCKS = (ATT_PAST_CHUNKS + ATT_Q_CHUNKS) // ATT_Q_CHUNKS
ATT_WINDOW = ATT_K_BLOCKS * ATT_Q_BLOCK
TOEPLITZ_COLS = 1024

W_A_COLS = IN_OFFSETS[2]
W_B_START = IN_OFFSETS[3]
W_B_COLS = IN_OFFSETS[10] - IN_OFFSETS[3]
WB_SB, WB_SC, WB_SH, WB_Q, WB_K, WB_V, WB_G = (IN_OFFSETS[i] - IN_OFFSETS[3] for i in range(3, 10))


def _cparams(semantics, vmem_mib):
    return pltpu.CompilerParams(dimension_semantics=semantics, vmem_limit_bytes=vmem_mib * MIB)


def _dot(a, b):
    return jnp.dot(a, b, preferred_element_type=F32)


def _dot_nt(a, b):
    return lax.dot_general(a, b, (((1,), (1,)), ((), ())), preferred_element_type=F32)


def _split_bf16(v, parts):
    out = []
    r = v
    for _ in range(parts):
        p = r.astype(BF16)
        out.append(p)
        r = r - p.astype(F32)
    return out


def _dot_exact_lhs(v, m, parts):
    acc = None
    for p in _split_bf16(v, parts):
        t = _dot(p, m)
        acc = t if acc is None else acc + t
    return acc


def _dot_exact_rhs(m, v, parts):
    acc = None
    for p in _split_bf16(v, parts):
        t = _dot(m, p)
        acc = t if acc is None else acc + t
    return acc


def _rmsnorm_kernel(x_ref, g_ref, o_ref):
    x = x_ref[...]
    ms = jnp.mean(x * x, axis=-1, keepdims=True)
    o_ref[...] = (x * lax.rsqrt(ms + EPS) * g_ref[...]).astype(o_ref.dtype)


def _rmsnorm(x, g, tm):
    t, d = x.shape
    return pl.pallas_call(
        _rmsnorm_kernel,
        grid=(t // tm,),
        in_specs=[pl.BlockSpec((tm, d), lambda i: (i, 0)), pl.BlockSpec((1, d), lambda i: (0, 0))],
        out_specs=pl.BlockSpec((tm, d), lambda i: (i, 0)),
        out_shape=jax.ShapeDtypeStruct((t, d), BF16),
        compiler_params=_cparams(("arbitrary",), 40),
        name="rmsnorm",
    )(x, g.reshape(1, d))


def _prep_w_in_kernel(w_ref, a_ref, dt_ref, b_ref):
    a_ref[...] = w_ref[0, :, 0:W_A_COLS].astype(BF16)
    d = w_ref[0, :, W_A_COLS:W_A_COLS + LANES]
    lane = lax.broadcasted_iota(jnp.int32, d.shape, 1)
    dt_ref[...] = jnp.where(lane < SSM_HEADS, d, 0.0).astype(BF16)
    b_ref[...] = w_ref[0, :, W_B_START:W_B_START + W_B_COLS].astype(BF16)


def _prep_w_in(w_in, layer, *, tr):
    _, k, n = w_in.shape
    row = lambda r: (r, 0)
    return pl.pallas_call(
        _prep_w_in_kernel,
        grid=(k // tr,),
        in_specs=[pl.BlockSpec((1, tr, n), lambda r: (layer, r, 0))],
        out_specs=[pl.BlockSpec((tr, W_A_COLS), row), pl.BlockSpec((tr, LANES), row),
                   pl.BlockSpec((tr, W_B_COLS), row)],
        out_shape=[jax.ShapeDtypeStruct((k, W_A_COLS), BF16), jax.ShapeDtypeStruct((k, LANES), BF16),
                   jax.ShapeDtypeStruct((k, W_B_COLS), BF16)],
        compiler_params=_cparams(("arbitrary",), 48),
        name="prep_w_in",
    )(w_in)


def _cast_kernel(w_ref, o_ref):
    o_ref[...] = w_ref[0].astype(o_ref.dtype)


def _cast_bf16(w, layer, *, tr):
    _, r, c = w.shape
    return pl.pallas_call(
        _cast_kernel,
        grid=(r // tr,),
        in_specs=[pl.BlockSpec((1, tr, c), lambda i: (layer, i, 0))],
        out_specs=pl.BlockSpec((tr, c), lambda i: (i, 0)),
        out_shape=jax.ShapeDtypeStruct((r, c), BF16),
        compiler_params=_cparams(("arbitrary",), 40),
        name="cast_bf16",
    )(w)


def _wspec(k, tn, col0):
    return pl.BlockSpec((k, tn), lambda j, i: (0, col0 // tn + j))


def _tail_spec(t, n, tm, tn, tail_rows, tiles_per_seq):
    if tail_rows == tm:
        return pl.BlockSpec((tm, tn), lambda j, i: (i, j)), jax.ShapeDtypeStruct((t, n), F32)
    nseq = t // (tm * tiles_per_seq)
    return (pl.BlockSpec((tail_rows, tn), lambda j, i: (i // tiles_per_seq, j)),
            jax.ShapeDtypeStruct((nseq * tail_rows, n), F32))


def _proj_act_kernel(*refs, act, has_bias, out_kinds):
    x_ref, w_ref = refs[0], refs[1]
    pos = 2
    acc = _dot(x_ref[...], w_ref[...])
    if has_bias:
        acc = acc + refs[pos][...]
        pos += 1
    y = act(acc)
    for kind, o_ref in zip(out_kinds, refs[pos:]):
        if kind == "tail":
            o_ref[...] = y[y.shape[0] - o_ref.shape[0]:]
        else:
            o_ref[...] = y.astype(o_ref.dtype)


def _proj_act(xn, w, col0, n, *, tm, tn, act, out_kinds, bias=None, tail_rows=None, tiles_per_seq=1, name):
    t, k = xn.shape
    in_specs = [pl.BlockSpec((tm, k), lambda j, i: (i, 0)), _wspec(k, tn, col0)]
    args = [xn, w]
    if bias is not None:
        in_specs.append(pl.BlockSpec((1, tn), lambda j, i: (0, j)))
        args.append(bias.reshape(1, n))
    out_specs, out_shape = [], []
    for kind in out_kinds:
        if kind == "tail":
            spec, shape = _tail_spec(t, n, tm, tn, tail_rows, tiles_per_seq)
        else:
            spec = pl.BlockSpec((tm, tn), lambda j, i: (i, j))
            shape = jax.ShapeDtypeStruct((t, n), F32 if kind == "f32" else BF16)
        out_specs.append(spec)
        out_shape.append(shape)
    return pl.pallas_call(
        functools.partial(_proj_act_kernel, act=act, has_bias=bias is not None, out_kinds=tuple(out_kinds)),
        grid=(n // tn, t // tm),
        in_specs=in_specs,
        out_specs=out_specs,
        out_shape=out_shape,
        compiler_params=_cparams(("arbitrary", "arbitrary"), 48),
        name=name,
    )(*args)


def _proj_headnorm_kernel(x_ref, w_ref, g_ref, *o_refs, scale, out_kinds):
    acc = _dot(x_ref[...], w_ref[...])
    g = g_ref[...]
    for h in range(acc.shape[1] // HEAD_DIM):
        sl = slice(h * HEAD_DIM, (h + 1) * HEAD_DIM)
        blk = acc[:, sl]
        ms = jnp.mean(blk * blk, axis=-1, keepdims=True)
        y = blk * lax.rsqrt(ms + EPS) * g
        for kind, o_ref in zip(out_kinds, o_refs):
            if kind == "tail":
                o_ref[:, sl] = y[y.shape[0] - o_ref.shape[0]:]
            else:
                o_ref[:, sl] = (y * scale).astype(BF16)


def _proj_headnorm(xn, w, col0, n, g, *, tm, tn, scale, out_kinds, tail_rows=None, tiles_per_seq=1, name):
    t, k = xn.shape
    out_specs, out_shape = [], []
    for kind in out_kinds:
        if kind == "tail":
            spec, shape = _tail_spec(t, n, tm, tn, tail_rows, tiles_per_seq)
        else:
            spec = pl.BlockSpec((tm, tn), lambda j, i: (i, j))
            shape = jax.ShapeDtypeStruct((t, n), BF16)
        out_specs.append(spec)
        out_shape.append(shape)
    return pl.pallas_call(
        functools.partial(_proj_headnorm_kernel, scale=scale, out_kinds=tuple(out_kinds)),
        grid=(n // tn, t // tm),
        in_specs=[pl.BlockSpec((tm, k), lambda j, i: (i, 0)), _wspec(k, tn, col0),
                  pl.BlockSpec((1, HEAD_DIM), lambda j, i: (0, 0))],
        out_specs=out_specs,
        out_shape=out_shape,
        compiler_params=_cparams(("arbitrary", "arbitrary"), 48),
        name=name,
    )(xn, w, g.reshape(1, HEAD_DIM))


def _causal_conv(u, cw_ref, p_ref, carry_ref, st_ref, i, *, width, nseg, tiles_per_seq):
    tm, tn = u.shape
    seg_len = tm // nseg
    row8 = lax.broadcasted_iota(jnp.int32, (SUBLANES, tn), 0)
    if tiles_per_seq > 1:
        @pl.when(lax.rem(i, tiles_per_seq) == 0)
        def _():
            carry_ref[...] = p_ref[0]
    outs = []
    for s in range(nseg):
        seg = u[s * seg_len:(s + 1) * seg_len]
        prev8 = carry_ref[...] if tiles_per_seq > 1 else p_ref[s]
        acc = cw_ref[width - 1:width, :] * seg
        for k in range(1, width):
            sh = pltpu.roll(seg, k, 0)
            first8 = jnp.where(row8 < k, pltpu.roll(prev8, k, 0), sh[0:SUBLANES])
            shk = jnp.concatenate([first8, sh[SUBLANES:]], axis=0)
            acc = acc + cw_ref[width - 1 - k:width - k, :] * shk
        outs.append(acc)
        st_ref[s] = seg[seg_len - SUBLANES:seg_len]
    if tiles_per_seq > 1:
        carry_ref[...] = u[tm - SUBLANES:tm]
    return outs[0] if nseg == 1 else jnp.concatenate(outs, axis=0)


def _proj_xbc_kernel(x_ref, w_ref, cw_ref, cb_ref, p_ref, o_ref, st_ref, carry_ref, *, nseg, tiles_per_seq):
    u = _dot(x_ref[...], w_ref[...])
    y = _causal_conv(u, cw_ref, p_ref, carry_ref, st_ref, pl.program_id(1), width=SSM_CONV, nseg=nseg,
                     tiles_per_seq=tiles_per_seq)
    y = y + cb_ref[...]
    o_ref[...] = y * jax.nn.sigmoid(y)


def _proj_sc_kernel(x_ref, wb_ref, wc_ref, wh_ref, cw_ref, p_ref, o_ref, st_ref, carry_ref, *, nseg,
                    tiles_per_seq):
    x = x_ref[...]
    u = _dot(x, wc_ref[...]) * _dot(x, wh_ref[...])
    v = _causal_conv(u, cw_ref, p_ref, carry_ref, st_ref, pl.program_id(1), width=SC_WIDTH, nseg=nseg,
                     tiles_per_seq=tiles_per_seq)
    o_ref[...] = (_dot(x, wb_ref[...]) * v).astype(o_ref.dtype)


def _proj_conv(kernel, xn, w, col0s, n, conv_w8, conv_b, prefix8, *, tm, tn, seq_len, out_dtype, name):
    t, k = xn.shape
    nseg = max(1, tm // seq_len)
    tiles_per_seq = max(1, seq_len // tm)
    if tiles_per_seq > 1:
        seq_map = lambda j, i: (i // tiles_per_seq, 0, j)
    else:
        seq_map = lambda j, i: (i, 0, j)
    in_specs = [pl.BlockSpec((tm, k), lambda j, i: (i, 0))]
    in_specs += [_wspec(k, tn, c) for c in col0s]
    in_specs.append(pl.BlockSpec((SUBLANES, tn), lambda j, i: (0, j)))
    args = [xn] + [w] * len(col0s) + [conv_w8]
    if conv_b is not None:
        in_specs.append(pl.BlockSpec((1, tn), lambda j, i: (0, j)))
        args.append(conv_b.reshape(1, n))
    in_specs.append(pl.BlockSpec((nseg, SUBLANES, tn), seq_map))
    args.append(prefix8)
    y, tails = pl.pallas_call(
        functools.partial(kernel, nseg=nseg, tiles_per_seq=tiles_per_seq),
        grid=(n // tn, t // tm),
        in_specs=in_specs,
        out_specs=[pl.BlockSpec((tm, tn), lambda j, i: (i, j)),
                   pl.BlockSpec((nseg, SUBLANES, tn), lambda j, i: (i, 0, j))],
        out_shape=[jax.ShapeDtypeStruct((t, n), out_dtype),
                   jax.ShapeDtypeStruct((t // tm * nseg, SUBLANES, n), F32)],
        scratch_shapes=[pltpu.VMEM((SUBLANES, tn), F32)],
        compiler_params=_cparams(("arbitrary", "arbitrary"), 48),
        name=name,
    )(*args)
    return y, tails[tiles_per_seq - 1::tiles_per_seq]


def _ssd_kernel(x_ref, b_ref, c_ref, dt_ref, zs_ref, alog_ref, dx_ref, ng_ref, ech_ref, els_ref, h0_ref,
                y_ref, hl_ref, ht_ref, *, L, nc):
    ci = pl.program_id(1)
    hp = LANES // L
    ntiles = SSM_HEADS // hp
    gw = SSM_GROUP_DIM

    @pl.when(ci == 0)
    def _():
        ht_ref[...] = h0_ref[0].T

    x = x_ref[...]
    bm = b_ref[...]
    cm = c_ref[...]
    dt = dt_ref[...]
    a = -jnp.exp(alog_ref[...])
    da = dt * a

    ri = lax.broadcasted_iota(jnp.int32, (L, L), 0)
    cj = lax.broadcasted_iota(jnp.int32, (L, L), 1)
    tri = (ri >= cj).astype(BF16)
    acum = _dot_exact_rhs(tri, da, 3)
    eacum = jnp.exp(acum)
    dend = jnp.exp(acum[L - 1:L, :] - acum)
    w = dt * dend

    ech = ech_ref[...]
    cexp = _dot_exact_lhs(acum, els_ref[...], 3)
    wx = _dot_exact_lhs(w, ech, 2)
    ex = _dot_exact_lhs(eacum, ech, 2)

    tile_rows = lambda v: jnp.concatenate([v] * hp, axis=0)
    acum_t = tile_rows(acum).T
    dt_t = tile_rows(dt).T

    lane = lax.broadcasted_iota(jnp.int32, (1, LANES), 1)
    log2_l = L.bit_length() - 1
    lane_blk = jnp.right_shift(lane, log2_l)
    row_l = lax.broadcasted_iota(jnp.int32, (L, LANES), 0)
    lane_s = jnp.bitwise_and(lax.broadcasted_iota(jnp.int32, (L, LANES), 1), L - 1)
    causal = row_l >= lane_s

    bsq = [tile_rows(bm[:, g * SSM_STATE:(g + 1) * SSM_STATE]) for g in range(SSM_GROUPS)]
    cbt = [_dot_nt(cm[:, g * SSM_STATE:(g + 1) * SSM_STATE].astype(BF16), bsq[g].astype(BF16))
           for g in range(SSM_GROUPS)]

    tw = hp * SSM_HEAD_DIM
    rb = jnp.right_shift(lax.broadcasted_iota(jnp.int32, (LANES, tw), 0), log2_l)
    cb_ = jnp.right_shift(lax.broadcasted_iota(jnp.int32, (LANES, tw), 1), SSM_HEAD_DIM.bit_length() - 1)
    blockdiag = rb == cb_

    yd = []
    for t in range(ntiles):
        h_first = t * hp
        g = h_first // (SSM_HEADS // SSM_GROUPS)
        r_row = acum_t[h_first:h_first + 1, :]
        d_row = dt_t[h_first:h_first + 1, :]
        for jj in range(1, hp):
            sel = lane_blk == jj
            r_row = jnp.where(sel, acum_t[h_first + jj:h_first + jj + 1, :], r_row)
            d_row = jnp.where(sel, dt_t[h_first + jj:h_first + jj + 1, :], d_row)
        diff = cexp[:, t * LANES:(t + 1) * LANES] - r_row
        dec = jnp.exp(jnp.where(causal, diff, -jnp.inf))
        sc = (cbt[g] * dec * d_row).astype(BF16)
        xs = tile_rows(x[:, t * tw:(t + 1) * tw])
        rhs = jnp.where(blockdiag, xs, 0.0).astype(BF16)
        yd.append(_dot(sc, rhs))
    y = jnp.concatenate(yd, axis=1)

    xw = x * wx
    zpad = jnp.zeros((LANES - L, gw), F32)
    for g in range(SSM_GROUPS):
        gs = slice(g * gw, (g + 1) * gw)
        h_in = ht_ref[:, gs]
        y_off = _dot(cm[:, g * SSM_STATE:(g + 1) * SSM_STATE].astype(BF16), h_in.astype(BF16))
        yg = y[:, gs] + y_off * ex[:, gs] + dx_ref[:, gs] * x[:, gs]
        yg = yg * zs_ref[:, gs]
        ms = jnp.mean(yg * yg, axis=-1, keepdims=True)
        y_ref[:, gs] = (yg * lax.rsqrt(ms + EPS) * ng_ref[:, gs]).astype(y_ref.dtype)
        bm_t = bsq[g].T.astype(BF16)
        upd = jnp.concatenate([xw[:, gs], zpad], axis=0).astype(BF16)
        ht_ref[:, gs] = h_in * ex[L - 1:L, gs] + _dot(bm_t, upd)

    @pl.when(ci == nc - 1)
    def _():
        hl_ref[0] = ht_ref[...].T


def _ssd(xbc_act, dt, zs, a_log128, d_x, norm_g, ech, els, h0, *, L, seq_len):
    t = xbc_act.shape[0]
    nseq = h0.shape[0]
    nc = seq_len // L
    rmap = lambda b, c: (b * nc + c, 0)
    cmap = lambda b, c: (0, 0)
    nb = SSM_INNER // SSM_BC
    return pl.pallas_call(
        functools.partial(_ssd_kernel, L=L, nc=nc),
        grid=(nseq, nc),
        in_specs=[pl.BlockSpec((L, SSM_INNER), rmap),
                  pl.BlockSpec((L, SSM_BC), lambda b, c: (b * nc + c, nb)),
                  pl.BlockSpec((L, SSM_BC), lambda b, c: (b * nc + c, nb + 1)),
                  pl.BlockSpec((L, LANES), rmap),
                  pl.BlockSpec((L, SSM_INNER), rmap),
                  pl.BlockSpec((1, LANES), cmap),
                  pl.BlockSpec((1, SSM_INNER), cmap),
                  pl.BlockSpec((1, SSM_INNER), cmap),
                  pl.BlockSpec(ech.shape, cmap),
                  pl.BlockSpec(els.shape, cmap),
                  pl.BlockSpec((1, SSM_INNER, SSM_STATE), lambda b, c: (b, 0, 0))],
        out_specs=[pl.BlockSpec((L, SSM_INNER), rmap),
                   pl.BlockSpec((1, SSM_INNER, SSM_STATE), lambda b, c: (b, 0, 0))],
        out_shape=[jax.ShapeDtypeStruct((t, SSM_INNER), BF16),
                   jax.ShapeDtypeStruct((nseq, SSM_INNER, SSM_STATE), F32)],
        scratch_shapes=[pltpu.VMEM((SSM_STATE, SSM_INNER), F32)],
        compiler_params=_cparams(("arbitrary", "arbitrary"), 48),
        name=f"ssd_L{L}",
    )(xbc_act, xbc_act, xbc_act, dt, zs, a_log128, d_x, norm_g, ech, els, h0)


def _softmax_pv(s_parts, v_parts):
    m = functools.reduce(jnp.maximum, [jnp.max(s, axis=1, keepdims=True) for s in s_parts])
    l = None
    o = None
    for s, v in zip(s_parts, v_parts):
        p = jnp.exp(s - m)
        ls = jnp.sum(p, axis=1, keepdims=True)
        os_ = _dot(p.astype(BF16), v)
        l = ls if l is None else l + ls
        o = os_ if o is None else o + os_
    return o / l


def _attn_prompt_kernel(q_ref, k0_ref, k1_ref, k2_ref, v0_ref, v1_ref, v2_ref, bias_ref, o_ref):
    i = pl.program_id(1)
    k_refs = (k0_ref, k1_ref, k2_ref)
    v_refs = (v0_ref, v1_ref, v2_ref)
    first_valid = (ATT_K_BLOCKS - 1 - i) * ATT_Q_BLOCK
    kidx = lax.broadcasted_iota(jnp.int32, (ATT_Q_BLOCK, ATT_Q_BLOCK), 1)
    for h in range(N_HEADS):
        sl = slice(h * HEAD_DIM, (h + 1) * HEAD_DIM)
        q = q_ref[:, sl]
        s_parts = []
        for kb in range(ATT_K_BLOCKS):
            s = _dot_nt(q, k_refs[kb][:, sl]) + bias_ref[h, :, kb * ATT_Q_BLOCK:(kb + 1) * ATT_Q_BLOCK]
            s_parts.append(jnp.where(kidx + kb * ATT_Q_BLOCK >= first_valid, s, NEG_INF))
        o = _softmax_pv(s_parts, [v_refs[kb][:, sl] for kb in range(ATT_K_BLOCKS)])
        o_ref[:, sl] = o.astype(o_ref.dtype)


def _attn_prompt(q, k, v, bias, *, nseq, seq_len):
    t = q.shape[0]
    nqb = seq_len // ATT_Q_BLOCK
    qmap = lambda b, i: (b * nqb + i, 0)

    def kmap(back):
        return lambda b, i: (b * nqb + jnp.maximum(i - back, 0), 0)

    blk = (ATT_Q_BLOCK, ATT_DIM)
    kv_specs = [pl.BlockSpec(blk, kmap(ATT_K_BLOCKS - 1 - kb)) for kb in range(ATT_K_BLOCKS)]
    return pl.pallas_call(
        _attn_prompt_kernel,
        grid=(nseq, nqb),
        in_specs=[pl.BlockSpec(blk, qmap)] + kv_specs + kv_specs
        + [pl.BlockSpec(bias.shape, lambda b, i: (0, 0, 0))],
        out_specs=pl.BlockSpec(blk, qmap),
        out_shape=jax.ShapeDtypeStruct((t, ATT_DIM), BF16),
        compiler_params=_cparams(("arbitrary", "arbitrary"), 56),
        name="attn_prompt",
    )(q, k, k, k, v, v, v, bias)


def _attn_sample_kernel(q_ref, kn_ref, vn_ref, kc_ref, vc_ref, bias_ref, o_ref):
    lc = kc_ref.shape[2]
    for h in range(N_HEADS):
        sl = slice(h * HEAD_DIM, (h + 1) * HEAD_DIM)
        q = q_ref[:, sl]
        s_c = _dot_nt(q, kc_ref[0, 0, :, sl].astype(BF16)) + bias_ref[h, :, 0:lc]
        s_n = _dot_nt(q, kn_ref[:, sl]) + bias_ref[h, :, lc:]
        o = _softmax_pv([s_c, s_n], [vc_ref[0, 0, :, sl].astype(BF16), vn_ref[:, sl]])
        o_ref[:, sl] = o.astype(o_ref.dtype)


def _attn_sample(q, k, v, k_cache, v_cache, layer, bias, *, nseq, seq_len):
    t = q.shape[0]
    lc = k_cache.shape[2]
    blk = (seq_len, ATT_DIM)
    rmap = lambda b: (b, 0)
    cspec = pl.BlockSpec((1, 1, lc, ATT_DIM), lambda b: (layer, b, 0, 0))
    return pl.pallas_call(
        _attn_sample_kernel,
        grid=(nseq,),
        in_specs=[pl.BlockSpec(blk, rmap), pl.BlockSpec(blk, rmap), pl.BlockSpec(blk, rmap), cspec, cspec,
                  pl.BlockSpec(bias.shape, lambda b: (0, 0, 0))],
        out_specs=pl.BlockSpec(blk, rmap),
        out_shape=jax.ShapeDtypeStruct((t, ATT_DIM), BF16),
        compiler_params=_cparams(("arbitrary",), 48),
        name="attn_sample",
    )(q, k, v, k_cache, v_cache, bias)


def _merge_kernel(ys_ref, yc_ref, ya_ref, ws_ref, wc_ref, wa_ref, g0_ref, g1_ref, g2_ref, o_ref, w16_ref):
    @pl.when(pl.program_id(1) == 0)
    def _():
        for b, w_ref in enumerate((ws_ref, wc_ref, wa_ref)):
            w16_ref[b] = w_ref[0].astype(BF16)

    m = g0_ref[...] * _dot(ys_ref[...], w16_ref[0])
    m = m + g1_ref[...] * _dot(yc_ref[...], w16_ref[1])
    m = m + g2_ref[...] * _dot(ya_ref[...], w16_ref[2])
    o_ref[...] = m.astype(o_ref.dtype)


def _merge(y_ssm, y_sc, y_att, w_ssm, w_sc, w_att, layer, gates, *, tm, tn):
    t, k = y_ssm.shape
    n = w_ssm.shape[2]
    ncol = n // tn
    lhs = pl.BlockSpec((tm, k), lambda j, i: (i, 0))
    rhs = pl.BlockSpec((1, k, tn), lambda j, i: (layer, 0, j))

    def gspec(b):
        return pl.BlockSpec((tm, tn), lambda j, i: (i, b * ncol + j))

    return pl.pallas_call(
        _merge_kernel,
        grid=(ncol, t // tm),
        in_specs=[lhs, lhs, lhs, rhs, rhs, rhs, gspec(0), gspec(1), gspec(2)],
        out_specs=pl.BlockSpec((tm, tn), lambda j, i: (i, j)),
        out_shape=jax.ShapeDtypeStruct((t, n), BF16),
        scratch_shapes=[pltpu.VMEM((N_BRANCH, k, tn), BF16)],
        compiler_params=_cparams(("arbitrary", "arbitrary"), 56),
        name="merge",
    )(y_ssm, y_sc, y_att, w_ssm, w_sc, w_att, gates, gates, gates)


def _wo_kernel(x_ref, m_ref, w_ref, g_ref, h_ref, hn_ref):
    h = x_ref[...] + _dot(m_ref[...], w_ref[...])
    h_ref[...] = h
    ms = jnp.mean(h * h, axis=-1, keepdims=True)
    hn_ref[...] = (h * lax.rsqrt(ms + EPS) * g_ref[...]).astype(hn_ref.dtype)


def _wo(x, merged, w_o, g, *, tm):
    t, d = x.shape
    row = lambda i: (i, 0)
    const = lambda i: (0, 0)
    return pl.pallas_call(
        _wo_kernel,
        grid=(t // tm,),
        in_specs=[pl.BlockSpec((tm, d), row), pl.BlockSpec((tm, d), row), pl.BlockSpec((d, d), const),
                  pl.BlockSpec((1, d), const)],
        out_specs=[pl.BlockSpec((tm, d), row), pl.BlockSpec((tm, d), row)],
        out_shape=[jax.ShapeDtypeStruct((t, d), F32), jax.ShapeDtypeStruct((t, d), BF16)],
        compiler_params=_cparams(("arbitrary",), 52),
        name="wo",
    )(x, merged, w_o, g.reshape(1, d))


def _ffn_kernel(h_ref, hn_ref, w1_ref, w2_ref, o_ref):
    c = pl.program_id(1)
    a = _dot(hn_ref[...], w1_ref[...])
    a = jnp.square(jnp.maximum(a, 0.0)).astype(BF16)
    contrib = _dot(a, w2_ref[...])

    @pl.when(c == 0)
    def _():
        o_ref[...] = h_ref[...] + contrib

    @pl.when(c > 0)
    def _():
        o_ref[...] += contrib


def _ffn(h, hn, w1, w2, *, tm, tc):
    t, d = h.shape
    dff = w1.shape[1]
    row = lambda i, c: (i, 0)
    return pl.pallas_call(
        _ffn_kernel,
        grid=(t // tm, dff // tc),
        in_specs=[pl.BlockSpec((tm, d), row), pl.BlockSpec((tm, d), row),
                  pl.BlockSpec((d, tc), lambda i, c: (0, c)), pl.BlockSpec((tc, d), lambda i, c: (c, 0))],
        out_specs=pl.BlockSpec((tm, d), row),
        out_shape=jax.ShapeDtypeStruct((t, d), F32),
        compiler_params=_cparams(("arbitrary", "arbitrary"), 52),
        name="ffn",
    )(h, hn, w1, w2)


def _head_expand(lanes_per_head):
    rows = lax.broadcasted_iota(jnp.int32, (LANES, SSM_HEADS * lanes_per_head), 0)
    cols = lax.broadcasted_iota(jnp.int32, (LANES, SSM_HEADS * lanes_per_head), 1) // lanes_per_head
    return (rows == cols).astype(BF16)


def _toeplitz_bias(rel_bias):
    tbl = rel_bias.astype(F32)
    far = jnp.broadcast_to(tbl[:, 2 * MAX_REL:], (N_HEADS, ATT_PAST - MAX_REL))
    wrap = jnp.broadcast_to(tbl[:, 2 * MAX_REL:], (N_HEADS, ATT_Q_BLOCK - 1))
    n_near = TOEPLITZ_COLS + 1 - far.shape[1] - tbl.shape[1] - wrap.shape[1]
    near = jnp.broadcast_to(tbl[:, :1], (N_HEADS, n_near))
    period = jnp.concatenate([far, tbl[:, ::-1], near, wrap], axis=1)
    assert period.shape[1] == TOEPLITZ_COLS + 1
    flat = jnp.tile(period, (1, ATT_Q_BLOCK))[:, :ATT_Q_BLOCK * TOEPLITZ_COLS]
    return flat.reshape(N_HEADS, ATT_Q_BLOCK, TOEPLITZ_COLS)


def _prompt_bias(toeplitz):
    qi = jnp.arange(ATT_Q_BLOCK)[:, None]
    kj = jnp.arange(ATT_WINDOW)[None, :]
    band = kj - (qi // CHUNK) * CHUNK
    in_band = (band >= 0) & (band < ATT_BAND)
    return jnp.where(in_band[None], toeplitz[:, :, :ATT_WINDOW], NEG_INF)


def _pad_rows_to8(a, axis):
    pad = [(0, 0)] * a.ndim
    pad[axis] = (SUBLANES - a.shape[axis], 0)
    return jnp.pad(a, pad)


def _layer(x, lw, layer, *, nseq, seq_len, tm, ssd_chunk, ssm_conv_prefix, ssm_h0, sc_prefix, kv_cache):
    xn = _rmsnorm(x, lw["norm_mix_g"], tm)
    tn = 1024 if tm >= 1024 else 2048
    wa, wb = lw["w_a"], lw["w_b"]
    tiles_per_seq = max(1, seq_len // tm)
    tail_rows = min(tm, min(ATT_PAST, seq_len))
    if tm > seq_len:
        tail_rows = tm

    (zs,) = _proj_act(xn, wa, 0, SSM_INNER, tm=tm, tn=min(tn, SSM_INNER), act=lambda a: a * jax.nn.sigmoid(a),
                      out_kinds=["f32"], name="proj_z")
    (dt,) = _proj_act(xn, lw["w_dt"], 0, LANES, tm=tm, tn=LANES, act=jax.nn.softplus, out_kinds=["f32"],
                      bias=lw["dt_bias128"], name="proj_dt")
    (gates,) = _proj_act(xn, wb, WB_G, N_BRANCH * D_MODEL, tm=tm, tn=tn, act=jax.nn.sigmoid,
                         out_kinds=["f32"], name="proj_gates")
    v16, v_tail = _proj_act(xn, wb, WB_V, ATT_DIM, tm=tm, tn=tn, act=lambda a: a, out_kinds=["bf16", "tail"],
                            tail_rows=tail_rows, tiles_per_seq=tiles_per_seq, name="proj_v")
    (q16,) = _proj_headnorm(xn, wb, WB_Q, ATT_DIM, lw["q_norm_g"], tm=tm, tn=tn, scale=HEAD_DIM ** -0.5,
                            out_kinds=["bf16"], name="proj_q")
    k16, k_tail = _proj_headnorm(xn, wb, WB_K, ATT_DIM, lw["k_norm_g"], tm=tm, tn=tn, scale=1.0,
                                 out_kinds=["bf16", "tail"], tail_rows=tail_rows, tiles_per_seq=tiles_per_seq,
                                 name="proj_k")
    xbc_act, ssm_conv_state = _proj_conv(_proj_xbc_kernel, xn, wa, [SSM_INNER], SSM_CONV_DIM,
                                         lw["ssm_conv_w8"], lw["ssm_conv_b"], ssm_conv_prefix, tm=tm, tn=512,
                                         seq_len=seq_len, out_dtype=F32, name="proj_xbc")
    y_sc, sc_state = _proj_conv(_proj_sc_kernel, xn, wb, [WB_SB, WB_SC, WB_SH], SC_DIM, lw["sc_conv_w8"],
                                None, sc_prefix, tm=tm, tn=512, seq_len=seq_len, out_dtype=BF16,
                                name="proj_sc")

    els = lw["ech"] if ssd_chunk == SSM_HEAD_DIM else _head_expand(ssd_chunk)
    y_ssm, h_last = _ssd(xbc_act, dt, zs, lw["a_log128"], lw["d_x"], lw["ssm_norm_g"], lw["ech"], els, ssm_h0,
                         L=ssd_chunk, seq_len=seq_len)

    if kv_cache is None:
        o = _attn_prompt(q16, k16, v16, lw["prompt_bias"], nseq=nseq, seq_len=seq_len)
    else:
        o = _attn_sample(q16, k16, v16, kv_cache[0], kv_cache[1], layer, lw["sample_bias"], nseq=nseq,
                         seq_len=seq_len)

    mtm = min(tm, 512)
    merged = _merge(y_ssm, y_sc, o, lw["ssm_out_w"], lw["sc_out_w"], lw["attn_out_w"], layer, gates,
                    tm=mtm, tn=512)
    h, hn = _wo(x, merged, lw["w_o"], lw["norm_ffn_g"], tm=mtm)
    y = _ffn(h, hn, lw["ffn_w1"], lw["ffn_w2"], tm=mtm, tc=1024)
    return y, (k_tail, v_tail, h_last, ssm_conv_state, sc_state)


def kernel(x_prompt, x_sample, cache_attn_k, cache_attn_v, state_ssm, state_ssm_conv, state_short_conv,
           norm_mix_g, w_in, ssm_conv_w, ssm_conv_b, ssm_dt_bias, ssm_a_log, ssm_d, ssm_norm_g, ssm_out_w,
           sc_conv_w, sc_out_w, q_norm_g, k_norm_g, rel_bias, attn_out_w, w_o, norm_ffn_g, ffn_w1, ffn_w2):
    bp, lp, d = x_prompt.shape
    bs, ls, _ = x_sample.shape
    lc = cache_attn_k.shape[2]
    assert lc == ATT_PAST and lc + ls <= TOEPLITZ_COLS and ls <= ATT_Q_BLOCK
    ech = _head_expand(SSM_HEAD_DIM)

    yp = x_prompt.reshape(bp * lp, d)
    ys = x_sample.reshape(bs * ls, d)
    new_p, new_s = [], []
    for l in range(DEPTH):
        w_a, w_dt, w_b = _prep_w_in(w_in, l, tr=128)
        toeplitz = _toeplitz_bias(rel_bias[l])
        lw = {
            "norm_mix_g": norm_mix_g[l], "norm_ffn_g": norm_ffn_g[l],
            "w_a": w_a, "w_dt": w_dt, "w_b": w_b,
            "dt_bias128": jnp.pad(ssm_dt_bias[l].astype(F32), (0, LANES - SSM_HEADS)),
            "a_log128": jnp.pad(ssm_a_log[l].astype(F32), (0, LANES - SSM_HEADS)).reshape(1, LANES),
            "d_x": jnp.repeat(ssm_d[l].astype(F32), SSM_HEAD_DIM).reshape(1, SSM_INNER),
            "ssm_norm_g": ssm_norm_g[l].astype(F32).reshape(1, SSM_INNER),
            "ssm_conv_w8": jnp.pad(ssm_conv_w[l].astype(F32), ((0, SUBLANES - SSM_CONV), (0, 0))),
            "ssm_conv_b": ssm_conv_b[l].astype(F32),
            "sc_conv_w8": jnp.pad(sc_conv_w[l].astype(F32), ((0, SUBLANES - SC_WIDTH), (0, 0))),
            "q_norm_g": q_norm_g[l].astype(F32), "k_norm_g": k_norm_g[l].astype(F32),
            "ssm_out_w": ssm_out_w, "sc_out_w": sc_out_w, "attn_out_w": attn_out_w,
            "w_o": _cast_bf16(w_o, l, tr=1024),
            "ffn_w1": _cast_bf16(ffn_w1, l, tr=256), "ffn_w2": _cast_bf16(ffn_w2, l, tr=1024),
            "ech": ech,
            "prompt_bias": _prompt_bias(toeplitz),
            "sample_bias": toeplitz[:, :ls, :lc + ls],
        }
        yp, st_p = _layer(
            yp, lw, l, nseq=bp, seq_len=lp, tm=1024, ssd_chunk=CHUNK,
            ssm_conv_prefix=jnp.zeros((bp, SUBLANES, SSM_CONV_DIM), F32),
            ssm_h0=jnp.zeros((bp, SSM_INNER, SSM_STATE), F32),
            sc_prefix=jnp.zeros((bp, SUBLANES, SC_DIM), F32), kv_cache=None)
        ys, st_s = _layer(
            ys, lw, l, nseq=bs, seq_len=ls, tm=bs * ls, ssd_chunk=ls,
            ssm_conv_prefix=_pad_rows_to8(state_ssm_conv[l].astype(F32), 1),
            ssm_h0=state_ssm[l].astype(F32).reshape(bs, SSM_INNER, SSM_STATE),
            sc_prefix=_pad_rows_to8(state_short_conv[l].astype(F32), 1),
            kv_cache=(cache_attn_k.reshape(DEPTH, bs, lc, ATT_DIM), cache_attn_v.reshape(DEPTH, bs, lc, ATT_DIM)))
        new_p.append(st_p)
        new_s.append(st_s)

    keep = min(ATT_PAST, lp)

    def stack(states, fn):
        return jnp.stack([fn(s) for s in states])

    hshape = lambda b: (b, SSM_HEADS, SSM_HEAD_DIM, SSM_STATE)
    return (
        yp.reshape(bp, lp, d),
        ys.reshape(bs, ls, d),
        stack(new_p, lambda s: s[0].reshape(bp, keep, N_HEADS, HEAD_DIM)),
        stack(new_p, lambda s: s[1].reshape(bp, keep, N_HEADS, HEAD_DIM)),
        stack(new_s, lambda s: s[0].reshape(bs, ls, N_HEADS, HEAD_DIM)),
        stack(new_s, lambda s: s[1].reshape(bs, ls, N_HEADS, HEAD_DIM)),
        stack(new_p, lambda s: s[2].reshape(hshape(bp))),
        stack(new_s, lambda s: s[2].reshape(hshape(bs))),
        stack(new_p, lambda s: s[3][:, SUBLANES - (SSM_CONV - 1):]),
        stack(new_s, lambda s: s[3][:, SUBLANES - (SSM_CONV - 1):]),
        stack(new_p, lambda s: s[4][:, SUBLANES - (SC_WIDTH - 1):]),
        stack(new_s, lambda s: s[4][:, SUBLANES - (SC_WIDTH - 1):]),
    )
```

```python
import functools
import math

import jax
import jax.numpy as jnp
from jax import lax
from jax.experimental import pallas as pl
from jax.experimental.pallas import tpu as pltpu

F32 = jnp.float32
BF16 = jnp.bfloat16

D_MODEL = 2048
DEPTH = 2
CHUNK = 64
EPS = 1e-6

SSM_INNER = D_MODEL
SSM_HEAD_DIM = 64
SSM_HEADS = SSM_INNER // SSM_HEAD_DIM
SSM_GROUPS = 4
SSM_STATE = 128
SSM_CONV = 4
SSM_BC = SSM_GROUPS * SSM_STATE
SSM_CONV_DIM = SSM_INNER + 2 * SSM_BC
SSM_GROUP_DIM = SSM_INNER // SSM_GROUPS

SC_DIM = D_MODEL
SC_WIDTH = 3

N_HEADS = 16
HEAD_DIM = D_MODEL // N_HEADS
ATT_DIM = N_HEADS * HEAD_DIM
ATT_PAST_CHUNKS = 8
ATT_PAST = ATT_PAST_CHUNKS * CHUNK
ATT_BAND = (ATT_PAST_CHUNKS + 1) * CHUNK
MAX_REL = 128

N_BRANCH = 3
D_FF = 4 * D_MODEL

IN_SPLITS = (SSM_INNER, SSM_CONV_DIM, SSM_HEADS, SC_DIM, SC_DIM, SC_DIM, ATT_DIM, ATT_DIM, ATT_DIM,
             N_BRANCH * D_MODEL)
IN_OFFSETS = tuple(int(sum(IN_SPLITS[:i])) for i in range(len(IN_SPLITS) + 1))

NEG_INF = -1e30

LANES = 128
SUBLANES = 8
MIB = 1024 * 1024

ATT_Q_CHUNKS = 4
ATT_Q_BLOCK = ATT_Q_CHUNKS * CHUNK
ATT_K_BLOCKS = (ATT_PAST_CHUNKS + ATT_Q_CHUNKS) // ATT_Q_CHUNKS
ATT_WINDOW = ATT_K_BLOCKS * ATT_Q_BLOCK
TOEPLITZ_COLS = 1024


def _cparams(semantics, vmem_mib):
    return pltpu.CompilerParams(dimension_semantics=semantics, vmem_limit_bytes=vmem_mib * MIB)


def _dot(a, b):
    return jnp.dot(a, b, preferred_element_type=F32)


def _dot_nt(a, b):
    return lax.dot_general(a, b, (((1,), (1,)), ((), ())), preferred_element_type=F32)


def _split_bf16(v, parts):
    out = []
    r = v
    for _ in range(parts):
        p = r.astype(BF16)
        out.append(p)
        r = r - p.astype(F32)
    return out


def _dot_exact_lhs(v, m, parts):
    acc = None
    for p in _split_bf16(v, parts):
        t = _dot(p, m)
        acc = t if acc is None else acc + t
    return acc


def _dot_exact_rhs(m, v, parts):
    acc = None
    for p in _split_bf16(v, parts):
        t = _dot(m, p)
        acc = t if acc is None else acc + t
    return acc


def _rmsnorm_kernel(x_ref, g_ref, o_ref):
    x = x_ref[...]
    ms = jnp.mean(x * x, axis=-1, keepdims=True)
    o_ref[...] = (x * lax.rsqrt(ms + EPS) * g_ref[...]).astype(o_ref.dtype)


def _rmsnorm(x, g, tm):
    t, d = x.shape
    return pl.pallas_call(
        _rmsnorm_kernel,
        grid=(t // tm,),
        in_specs=[pl.BlockSpec((tm, d), lambda i: (i, 0)), pl.BlockSpec((1, d), lambda i: (0, 0))],
        out_specs=pl.BlockSpec((tm, d), lambda i: (i, 0)),
        out_shape=jax.ShapeDtypeStruct((t, d), BF16),
        compiler_params=_cparams(("arbitrary",), 40),
        name="rmsnorm",
    )(x, g.reshape(1, d))


def _cast_kernel(w_ref, o_ref):
    o_ref[...] = w_ref[0].astype(o_ref.dtype)


def _cast_bf16(w, layer, *, tr):
    _, r, c = w.shape
    return pl.pallas_call(
        _cast_kernel,
        grid=(r // tr,),
        in_specs=[pl.BlockSpec((1, tr, c), lambda i: (layer, i, 0))],
        out_specs=pl.BlockSpec((tr, c), lambda i: (i, 0)),
        out_shape=jax.ShapeDtypeStruct((r, c), BF16),
        compiler_params=_cparams(("arbitrary",), 40),
        name="cast_bf16",
    )(w)


def _wspec(k, tn, layer, row0):
    assert row0 % SUBLANES == 0 and tn % SUBLANES == 0
    return pl.BlockSpec((pl.Element(1), pl.Element(tn), pl.Element(k)),
                        lambda j, i: (layer, pl.multiple_of(row0 + j * tn, SUBLANES), 0))


def _load_weight(w_ref, w16_ref, slot=None, valid_rows=None):
    @pl.when(pl.program_id(1) == 0)
    def _():
        w = w_ref[0]
        if valid_rows is not None:
            rows = lax.broadcasted_iota(jnp.int32, w.shape, 0)
            w = jnp.where(rows < valid_rows, w, 0.0)
        if slot is None:
            w16_ref[...] = w.astype(BF16)
        else:
            w16_ref[slot] = w.astype(BF16)


def _tail_spec(t, n, tm, tn, tail_rows, tiles_per_seq):
    if tail_rows == tm:
        return pl.BlockSpec((tm, tn), lambda j, i: (i, j)), jax.ShapeDtypeStruct((t, n), F32)
    nseq = t // (tm * tiles_per_seq)
    return (pl.BlockSpec((tail_rows, tn), lambda j, i: (i // tiles_per_seq, j)),
            jax.ShapeDtypeStruct((nseq * tail_rows, n), F32))


def _proj_act_kernel(*refs, act, has_bias, out_kinds, valid_rows):
    x_ref, w_ref, w16_ref = refs[0], refs[1], refs[-1]
    _load_weight(w_ref, w16_ref, valid_rows=valid_rows)
    pos = 2
    acc = _dot_nt(x_ref[...], w16_ref[...])
    if has_bias:
        acc = acc + refs[pos][...]
        pos += 1
    y = act(acc)
    for kind, o_ref in zip(out_kinds, refs[pos:-1]):
        if kind == "tail":
            o_ref[...] = y[y.shape[0] - o_ref.shape[0]:]
        else:
            o_ref[...] = y.astype(o_ref.dtype)


def _proj_act(xn, wt, layer, row0, n, *, tm, tn, act, out_kinds, bias=None, valid_rows=None, tail_rows=None,
              tiles_per_seq=1, name):
    t, k = xn.shape
    in_specs = [pl.BlockSpec((tm, k), lambda j, i: (i, 0)), _wspec(k, tn, layer, row0)]
    args = [xn, wt]
    if bias is not None:
        in_specs.append(pl.BlockSpec((1, tn), lambda j, i: (0, j)))
        args.append(bias.reshape(1, n))
    out_specs, out_shape = [], []
    for kind in out_kinds:
        if kind == "tail":
            spec, shape = _tail_spec(t, n, tm, tn, tail_rows, tiles_per_seq)
        else:
            spec = pl.BlockSpec((tm, tn), lambda j, i: (i, j))
            shape = jax.ShapeDtypeStruct((t, n), F32 if kind == "f32" else BF16)
        out_specs.append(spec)
        out_shape.append(shape)
    return pl.pallas_call(
        functools.partial(_proj_act_kernel, act=act, has_bias=bias is not None, out_kinds=tuple(out_kinds),
                          valid_rows=valid_rows),
        grid=(n // tn, t // tm),
        in_specs=in_specs,
        out_specs=out_specs,
        out_shape=out_shape,
        scratch_shapes=[pltpu.VMEM((tn, k), BF16)],
        compiler_params=_cparams(("arbitrary", "arbitrary"), 56),
        name=name,
    )(*args)


def _proj_headnorm_kernel(x_ref, w_ref, g_ref, *refs, scale, out_kinds):
    o_refs, w16_ref = refs[:-1], refs[-1]
    _load_weight(w_ref, w16_ref)
    acc = _dot_nt(x_ref[...], w16_ref[...])
    g = g_ref[...]
    for h in range(acc.shape[1] // HEAD_DIM):
        sl = slice(h * HEAD_DIM, (h + 1) * HEAD_DIM)
        blk = acc[:, sl]
        ms = jnp.mean(blk * blk, axis=-1, keepdims=True)
        y = blk * lax.rsqrt(ms + EPS) * g
        for kind, o_ref in zip(out_kinds, o_refs):
            if kind == "tail":
                o_ref[:, sl] = y[y.shape[0] - o_ref.shape[0]:]
            else:
                o_ref[:, sl] = (y * scale).astype(BF16)


def _proj_headnorm(xn, wt, layer, row0, n, g, *, tm, tn, scale, out_kinds, tail_rows=None, tiles_per_seq=1,
                   name):
    t, k = xn.shape
    out_specs, out_shape = [], []
    for kind in out_kinds:
        if kind == "tail":
            spec, shape = _tail_spec(t, n, tm, tn, tail_rows, tiles_per_seq)
        else:
            spec = pl.BlockSpec((tm, tn), lambda j, i: (i, j))
            shape = jax.ShapeDtypeStruct((t, n), BF16)
        out_specs.append(spec)
        out_shape.append(shape)
    return pl.pallas_call(
        functools.partial(_proj_headnorm_kernel, scale=scale, out_kinds=tuple(out_kinds)),
        grid=(n // tn, t // tm),
        in_specs=[pl.BlockSpec((tm, k), lambda j, i: (i, 0)), _wspec(k, tn, layer, row0),
                  pl.BlockSpec((1, HEAD_DIM), lambda j, i: (0, 0))],
        out_specs=out_specs,
        out_shape=out_shape,
        scratch_shapes=[pltpu.VMEM((tn, k), BF16)],
        compiler_params=_cparams(("arbitrary", "arbitrary"), 56),
        name=name,
    )(xn, wt, g.reshape(1, HEAD_DIM))


CONV_SUB = 256


def _conv_carry_init(p_ref, carry_ref, tiles_per_seq):
    if tiles_per_seq > 1:
        @pl.when(lax.rem(pl.program_id(1), tiles_per_seq) == 0)
        def _():
            carry_ref[...] = p_ref[0]


def _causal_conv(u, cs, cw_ref, p_ref, carry_ref, st_ref, *, width, nseg, tiles_per_seq):
    tm, tn = u.shape
    seg_len = tm // nseg
    row8 = lax.broadcasted_iota(jnp.int32, (SUBLANES, tn), 0)
    outs = []
    for s in range(nseg):
        seg = u[s * seg_len:(s + 1) * seg_len]
        prev8 = carry_ref[:, cs] if tiles_per_seq > 1 else p_ref[s, :, cs]
        acc = cw_ref[width - 1:width, cs] * seg
        for k in range(1, width):
            sh = pltpu.roll(seg, k, 0)
            first8 = jnp.where(row8 < k, pltpu.roll(prev8, k, 0), sh[0:SUBLANES])
            shk = jnp.concatenate([first8, sh[SUBLANES:]], axis=0)
            acc = acc + cw_ref[width - 1 - k:width - k, cs] * shk
        outs.append(acc)
        st_ref[s, :, cs] = seg[seg_len - SUBLANES:seg_len]
    if tiles_per_seq > 1:
        carry_ref[:, cs] = u[tm - SUBLANES:tm]
    return outs[0] if nseg == 1 else jnp.concatenate(outs, axis=0)


def _proj_xbc_kernel(x_ref, w_ref, cw_ref, cb_ref, p_ref, o_ref, st_ref, carry_ref, w16_ref, *, nseg,
                     tiles_per_seq):
    _load_weight(w_ref, w16_ref, slot=0)
    _conv_carry_init(p_ref, carry_ref, tiles_per_seq)
    x = x_ref[...]
    for c in range(o_ref.shape[1] // CONV_SUB):
        cs = slice(c * CONV_SUB, (c + 1) * CONV_SUB)
        u = _dot_nt(x, w16_ref[0, cs, :])
        y = _causal_conv(u, cs, cw_ref, p_ref, carry_ref, st_ref, width=SSM_CONV, nseg=nseg,
                         tiles_per_seq=tiles_per_seq)
        y = y + cb_ref[:, cs]
        o_ref[:, cs] = y * jax.nn.sigmoid(y)


def _proj_sc_kernel(x_ref, wb_ref, wc_ref, wh_ref, cw_ref, p_ref, o_ref, st_ref, carry_ref, w16_ref, *, nseg,
                    tiles_per_seq):
    for slot, w_ref in enumerate((wb_ref, wc_ref, wh_ref)):
        _load_weight(w_ref, w16_ref, slot=slot)
    _conv_carry_init(p_ref, carry_ref, tiles_per_seq)
    x = x_ref[...]
    for c in range(o_ref.shape[1] // CONV_SUB):
        cs = slice(c * CONV_SUB, (c + 1) * CONV_SUB)
        u = _dot_nt(x, w16_ref[1, cs, :]) * _dot_nt(x, w16_ref[2, cs, :])
        v = _causal_conv(u, cs, cw_ref, p_ref, carry_ref, st_ref, width=SC_WIDTH, nseg=nseg,
                         tiles_per_seq=tiles_per_seq)
        o_ref[:, cs] = (_dot_nt(x, w16_ref[0, cs, :]) * v).astype(o_ref.dtype)


def _proj_conv(kernel, xn, wt, layer, row0s, n, conv_w8, conv_b, prefix8, *, tm, tn, seq_len, out_dtype, name):
    t, k = xn.shape
    nseg = max(1, tm // seq_len)
    tiles_per_seq = max(1, seq_len // tm)
    if tiles_per_seq > 1:
        seq_map = lambda j, i: (i // tiles_per_seq, 0, j)
    else:
        seq_map = lambda j, i: (i, 0, j)
    in_specs = [pl.BlockSpec((tm, k), lambda j, i: (i, 0))]
    in_specs += [_wspec(k, tn, layer, r) for r in row0s]
    in_specs.append(pl.BlockSpec((SUBLANES, tn), lambda j, i: (0, j)))
    args = [xn] + [wt] * len(row0s) + [conv_w8]
    if conv_b is not None:
        in_specs.append(pl.BlockSpec((1, tn), lambda j, i: (0, j)))
        args.append(conv_b.reshape(1, n))
    in_specs.append(pl.BlockSpec((nseg, SUBLANES, tn), seq_map))
    args.append(prefix8)
    y, tails = pl.pallas_call(
        functools.partial(kernel, nseg=nseg, tiles_per_seq=tiles_per_seq),
        grid=(n // tn, t // tm),
        in_specs=in_specs,
        out_specs=[pl.BlockSpec((tm, tn), lambda j, i: (i, j)),
                   pl.BlockSpec((nseg, SUBLANES, tn), lambda j, i: (i, 0, j))],
        out_shape=[jax.ShapeDtypeStruct((t, n), out_dtype),
                   jax.ShapeDtypeStruct((t // tm * nseg, SUBLANES, n), F32)],
        scratch_shapes=[pltpu.VMEM((SUBLANES, tn), F32), pltpu.VMEM((len(row0s), tn, k), BF16)],
        compiler_params=_cparams(("arbitrary", "arbitrary"), 56),
        name=name,
    )(*args)
    return y, tails[tiles_per_seq - 1::tiles_per_seq]


def _ssd_kernel(x_ref, b_ref, c_ref, dt_ref, zs_ref, alog_ref, dx_ref, ng_ref, ech_ref, els_ref, h0_ref,
                y_ref, hl_ref, ht_ref, *, L, nc):
    ci = pl.program_id(1)
    hp = LANES // L
    ntiles = SSM_HEADS // hp
    gw = SSM_GROUP_DIM

    @pl.when(ci == 0)
    def _():
        ht_ref[...] = h0_ref[0].T

    x = x_ref[...]
    bm = b_ref[...]
    cm = c_ref[...]
    dt = dt_ref[...]
    a = -jnp.exp(alog_ref[...])
    da = dt * a

    ri = lax.broadcasted_iota(jnp.int32, (L, L), 0)
    cj = lax.broadcasted_iota(jnp.int32, (L, L), 1)
    tri = (ri >= cj).astype(BF16)
    acum = _dot_exact_rhs(tri, da, 3)
    eacum = jnp.exp(acum)
    dend = jnp.exp(acum[L - 1:L, :] - acum)
    w = dt * dend

    ech = ech_ref[...]
    cexp = _dot_exact_lhs(acum, els_ref[...], 3)
    wx = _dot_exact_lhs(w, ech, 2)
    ex = _dot_exact_lhs(eacum, ech, 2)

    tile_rows = lambda v: jnp.concatenate([v] * hp, axis=0)
    acum_t = tile_rows(acum).T
    dt_t = tile_rows(dt).T

    lane = lax.broadcasted_iota(jnp.int32, (1, LANES), 1)
    log2_l = L.bit_length() - 1
    lane_blk = jnp.right_shift(lane, log2_l)
    row_l = lax.broadcasted_iota(jnp.int32, (L, LANES), 0)
    lane_s = jnp.bitwise_and(lax.broadcasted_iota(jnp.int32, (L, LANES), 1), L - 1)
    causal = row_l >= lane_s

    bsq = [tile_rows(bm[:, g * SSM_STATE:(g + 1) * SSM_STATE]) for g in range(SSM_GROUPS)]
    cbt = [_dot_nt(cm[:, g * SSM_STATE:(g + 1) * SSM_STATE].astype(BF16), bsq[g].astype(BF16))
           for g in range(SSM_GROUPS)]

    tw = hp * SSM_HEAD_DIM
    rb = jnp.right_shift(lax.broadcasted_iota(jnp.int32, (LANES, tw), 0), log2_l)
    cb_ = jnp.right_shift(lax.broadcasted_iota(jnp.int32, (LANES, tw), 1), SSM_HEAD_DIM.bit_length() - 1)
    blockdiag = rb == cb_

    yd = []
    for t in range(ntiles):
        h_first = t * hp
        g = h_first // (SSM_HEADS // SSM_GROUPS)
        r_row = acum_t[h_first:h_first + 1, :]
        d_row = dt_t[h_first:h_first + 1, :]
        for jj in range(1, hp):
            sel = lane_blk == jj
            r_row = jnp.where(sel, acum_t[h_first + jj:h_first + jj + 1, :], r_row)
            d_row = jnp.where(sel, dt_t[h_first + jj:h_first + jj + 1, :], d_row)
        diff = cexp[:, t * LANES:(t + 1) * LANES] - r_row
        dec = jnp.exp(jnp.where(causal, diff, -jnp.inf))
        sc = (cbt[g] * dec * d_row).astype(BF16)
        xs = tile_rows(x[:, t * tw:(t + 1) * tw])
        rhs = jnp.where(blockdiag, xs, 0.0).astype(BF16)
        yd.append(_dot(sc, rhs))
    y = jnp.concatenate(yd, axis=1)

    xw = x * wx
    zpad = jnp.zeros((LANES - L, gw), F32)
    for g in range(SSM_GROUPS):
        gs = slice(g * gw, (g + 1) * gw)
        h_in = ht_ref[:, gs]
        y_off = _dot(cm[:, g * SSM_STATE:(g + 1) * SSM_STATE].astype(BF16), h_in.astype(BF16))
        yg = y[:, gs] + y_off * ex[:, gs] + dx_ref[:, gs] * x[:, gs]
        yg = yg * zs_ref[:, gs]
        ms = jnp.mean(yg * yg, axis=-1, keepdims=True)
        y_ref[:, gs] = (yg * lax.rsqrt(ms + EPS) * ng_ref[:, gs]).astype(y_ref.dtype)
        bm_t = bsq[g].T.astype(BF16)
        upd = jnp.concatenate([xw[:, gs], zpad], axis=0).astype(BF16)
        ht_ref[:, gs] = h_in * ex[L - 1:L, gs] + _dot(bm_t, upd)

    @pl.when(ci == nc - 1)
    def _():
        hl_ref[0] = ht_ref[...].T


def _ssd(xbc_act, dt, zs, a_log128, d_x, norm_g, ech, els, h0, *, L, seq_len):
    t = xbc_act.shape[0]
    nseq = h0.shape[0]
    nc = seq_len // L
    rmap = lambda b, c: (b * nc + c, 0)
    cmap = lambda b, c: (0, 0)
    nb = SSM_INNER // SSM_BC
    return pl.pallas_call(
        functools.partial(_ssd_kernel, L=L, nc=nc),
        grid=(nseq, nc),
        in_specs=[pl.BlockSpec((L, SSM_INNER), rmap),
                  pl.BlockSpec((L, SSM_BC), lambda b, c: (b * nc + c, nb)),
                  pl.BlockSpec((L, SSM_BC), lambda b, c: (b * nc + c, nb + 1)),
                  pl.BlockSpec((L, LANES), rmap),
                  pl.BlockSpec((L, SSM_INNER), rmap),
                  pl.BlockSpec((1, LANES), cmap),
                  pl.BlockSpec((1, SSM_INNER), cmap),
                  pl.BlockSpec((1, SSM_INNER), cmap),
                  pl.BlockSpec(ech.shape, cmap),
                  pl.BlockSpec(els.shape, cmap),
                  pl.BlockSpec((1, SSM_INNER, SSM_STATE), lambda b, c: (b, 0, 0))],
        out_specs=[pl.BlockSpec((L, SSM_INNER), rmap),
                   pl.BlockSpec((1, SSM_INNER, SSM_STATE), lambda b, c: (b, 0, 0))],
        out_shape=[jax.ShapeDtypeStruct((t, SSM_INNER), BF16),
                   jax.ShapeDtypeStruct((nseq, SSM_INNER, SSM_STATE), F32)],
        scratch_shapes=[pltpu.VMEM((SSM_STATE, SSM_INNER), F32)],
        compiler_params=_cparams(("arbitrary", "arbitrary"), 48),
        name=f"ssd_L{L}",
    )(xbc_act, xbc_act, xbc_act, dt, zs, a_log128, d_x, norm_g, ech, els, h0)


def _softmax_pv(s_parts, v_parts):
    m = functools.reduce(jnp.maximum, [jnp.max(s, axis=1, keepdims=True) for s in s_parts])
    l = None
    o = None
    for s, v in zip(s_parts, v_parts):
        p = jnp.exp(s - m)
        ls = jnp.sum(p, axis=1, keepdims=True)
        os_ = _dot(p.astype(BF16), v)
        l = ls if l is None else l + ls
        o = os_ if o is None else o + os_
    return o / l


def _attn_prompt_kernel(q_ref, k0_ref, k1_ref, k2_ref, v0_ref, v1_ref, v2_ref, bias_ref, o_ref):
    i = pl.program_id(1)
    k_refs = (k0_ref, k1_ref, k2_ref)
    v_refs = (v0_ref, v1_ref, v2_ref)
    first_valid = (ATT_K_BLOCKS - 1 - i) * ATT_Q_BLOCK
    kidx = lax.broadcasted_iota(jnp.int32, (ATT_Q_BLOCK, ATT_Q_BLOCK), 1)
    for h in range(N_HEADS):
        sl = slice(h * HEAD_DIM, (h + 1) * HEAD_DIM)
        q = q_ref[:, sl]
        s_parts = []
        for kb in range(ATT_K_BLOCKS):
            s = _dot_nt(q, k_refs[kb][:, sl]) + bias_ref[h, :, kb * ATT_Q_BLOCK:(kb + 1) * ATT_Q_BLOCK]
            s_parts.append(jnp.where(kidx + kb * ATT_Q_BLOCK >= first_valid, s, NEG_INF))
        o = _softmax_pv(s_parts, [v_refs[kb][:, sl] for kb in range(ATT_K_BLOCKS)])
        o_ref[:, sl] = o.astype(o_ref.dtype)


def _attn_prompt(q, k, v, bias, *, nseq, seq_len):
    t = q.shape[0]
    nqb = seq_len // ATT_Q_BLOCK
    qmap = lambda b, i: (b * nqb + i, 0)

    def kmap(back):
        return lambda b, i: (b * nqb + jnp.maximum(i - back, 0), 0)

    blk = (ATT_Q_BLOCK, ATT_DIM)
    kv_specs = [pl.BlockSpec(blk, kmap(ATT_K_BLOCKS - 1 - kb)) for kb in range(ATT_K_BLOCKS)]
    return pl.pallas_call(
        _attn_prompt_kernel,
        grid=(nseq, nqb),
        in_specs=[pl.BlockSpec(blk, qmap)] + kv_specs + kv_specs
        + [pl.BlockSpec(bias.shape, lambda b, i: (0, 0, 0))],
        out_specs=pl.BlockSpec(blk, qmap),
        out_shape=jax.ShapeDtypeStruct((t, ATT_DIM), BF16),
        compiler_params=_cparams(("arbitrary", "arbitrary"), 56),
        name="attn_prompt",
    )(q, k, k, k, v, v, v, bias)


def _attn_sample_kernel(q_ref, kn_ref, vn_ref, kc_ref, vc_ref, bias_ref, o_ref):
    lc = kc_ref.shape[2]
    for h in range(N_HEADS):
        sl = slice(h * HEAD_DIM, (h + 1) * HEAD_DIM)
        q = q_ref[:, sl]
        s_c = _dot_nt(q, kc_ref[0, 0, :, sl]) + bias_ref[h, :, 0:lc]
        s_n = _dot_nt(q, kn_ref[:, sl]) + bias_ref[h, :, lc:]
        o = _softmax_pv([s_c, s_n], [vc_ref[0, 0, :, sl], vn_ref[:, sl]])
        o_ref[:, sl] = o.astype(o_ref.dtype)


def _attn_sample(q, k, v, k_cache, v_cache, layer, bias, *, nseq, seq_len):
    t = q.shape[0]
    lc = k_cache.shape[2]
    blk = (seq_len, ATT_DIM)
    rmap = lambda b: (b, 0)
    cspec = pl.BlockSpec((1, 1, lc, ATT_DIM), lambda b: (layer, b, 0, 0))
    return pl.pallas_call(
        _attn_sample_kernel,
        grid=(nseq,),
        in_specs=[pl.BlockSpec(blk, rmap), pl.BlockSpec(blk, rmap), pl.BlockSpec(blk, rmap), cspec, cspec,
                  pl.BlockSpec(bias.shape, lambda b: (0, 0, 0))],
        out_specs=pl.BlockSpec(blk, rmap),
        out_shape=jax.ShapeDtypeStruct((t, ATT_DIM), BF16),
        compiler_params=_cparams(("arbitrary",), 48),
        name="attn_sample",
    )(q, k, v, k_cache, v_cache, bias)


def _merge_kernel(ys_ref, yc_ref, ya_ref, ws_ref, wc_ref, wa_ref, g0_ref, g1_ref, g2_ref, o_ref, w16_ref):
    @pl.when(pl.program_id(1) == 0)
    def _():
        for b, w_ref in enumerate((ws_ref, wc_ref, wa_ref)):
            w16_ref[b] = w_ref[0].astype(BF16)

    m = g0_ref[...] * _dot(ys_ref[...], w16_ref[0])
    m = m + g1_ref[...] * _dot(yc_ref[...], w16_ref[1])
    m = m + g2_ref[...] * _dot(ya_ref[...], w16_ref[2])
    o_ref[...] = m.astype(o_ref.dtype)


def _merge(y_ssm, y_sc, y_att, w_ssm, w_sc, w_att, layer, gates, *, tm, tn):
    t, k = y_ssm.shape
    n = w_ssm.shape[2]
    ncol = n // tn
    lhs = pl.BlockSpec((tm, k), lambda j, i: (i, 0))
    rhs = pl.BlockSpec((1, k, tn), lambda j, i: (layer, 0, j))

    def gspec(b):
        return pl.BlockSpec((tm, tn), lambda j, i: (i, b * ncol + j))

    return pl.pallas_call(
        _merge_kernel,
        grid=(ncol, t // tm),
        in_specs=[lhs, lhs, lhs, rhs, rhs, rhs, gspec(0), gspec(1), gspec(2)],
        out_specs=pl.BlockSpec((tm, tn), lambda j, i: (i, j)),
        out_shape=jax.ShapeDtypeStruct((t, n), BF16),
        scratch_shapes=[pltpu.VMEM((N_BRANCH, k, tn), BF16)],
        compiler_params=_cparams(("arbitrary", "arbitrary"), 56),
        name="merge",
    )(y_ssm, y_sc, y_att, w_ssm, w_sc, w_att, gates, gates, gates)


def _wo_kernel(x_ref, m_ref, w_ref, g_ref, h_ref, hn_ref):
    h = x_ref[...] + _dot(m_ref[...], w_ref[...])
    h_ref[...] = h
    ms = jnp.mean(h * h, axis=-1, keepdims=True)
    hn_ref[...] = (h * lax.rsqrt(ms + EPS) * g_ref[...]).astype(hn_ref.dtype)


def _wo(x, merged, w_o, g, *, tm):
    t, d = x.shape
    row = lambda i: (i, 0)
    const = lambda i: (0, 0)
    return pl.pallas_call(
        _wo_kernel,
        grid=(t // tm,),
        in_specs=[pl.BlockSpec((tm, d), row), pl.BlockSpec((tm, d), row), pl.BlockSpec((d, d), const),
                  pl.BlockSpec((1, d), const)],
        out_specs=[pl.BlockSpec((tm, d), row), pl.BlockSpec((tm, d), row)],
        out_shape=[jax.ShapeDtypeStruct((t, d), F32), jax.ShapeDtypeStruct((t, d), BF16)],
        compiler_params=_cparams(("arbitrary",), 52),
        name="wo",
    )(x, merged, w_o, g.reshape(1, d))


def _ffn_kernel(h_ref, hn_ref, w1_ref, w2_ref, o_ref):
    c = pl.program_id(1)
    a = _dot(hn_ref[...], w1_ref[...])
    a = jnp.square(jnp.maximum(a, 0.0)).astype(BF16)
    contrib = _dot(a, w2_ref[...])

    @pl.when(c == 0)
    def _():
        o_ref[...] = h_ref[...] + contrib

    @pl.when(c > 0)
    def _():
        o_ref[...] += contrib


def _ffn(h, hn, w1, w2, *, tm, tc):
    t, d = h.shape
    dff = w1.shape[1]
    row = lambda i, c: (i, 0)
    return pl.pallas_call(
        _ffn_kernel,
        grid=(t // tm, dff // tc),
        in_specs=[pl.BlockSpec((tm, d), row), pl.BlockSpec((tm, d), row),
                  pl.BlockSpec((d, tc), lambda i, c: (0, c)), pl.BlockSpec((tc, d), lambda i, c: (c, 0))],
        out_specs=pl.BlockSpec((tm, d), row),
        out_shape=jax.ShapeDtypeStruct((t, d), F32),
        compiler_params=_cparams(("arbitrary", "arbitrary"), 52),
        name="ffn",
    )(h, hn, w1, w2)


def _head_expand(lanes_per_head):
    rows = lax.broadcasted_iota(jnp.int32, (LANES, SSM_HEADS * lanes_per_head), 0)
    cols = lax.broadcasted_iota(jnp.int32, (LANES, SSM_HEADS * lanes_per_head), 1) // lanes_per_head
    return (rows == cols).astype(BF16)


def _toeplitz_bias(rel_bias):
    tbl = rel_bias.astype(F32)
    far = jnp.broadcast_to(tbl[:, 2 * MAX_REL:], (N_HEADS, ATT_PAST - MAX_REL))
    wrap = jnp.broadcast_to(tbl[:, 2 * MAX_REL:], (N_HEADS, ATT_Q_BLOCK - 1))
    n_near = TOEPLITZ_COLS + 1 - far.shape[1] - tbl.shape[1] - wrap.shape[1]
    near = jnp.broadcast_to(tbl[:, :1], (N_HEADS, n_near))
    period = jnp.concatenate([far, tbl[:, ::-1], near, wrap], axis=1)
    assert period.shape[1] == TOEPLITZ_COLS + 1
    flat = jnp.tile(period, (1, ATT_Q_BLOCK))[:, :ATT_Q_BLOCK * TOEPLITZ_COLS]
    return flat.reshape(N_HEADS, ATT_Q_BLOCK, TOEPLITZ_COLS)


def _prompt_bias(toeplitz):
    qi = jnp.arange(ATT_Q_BLOCK)[:, None]
    kj = jnp.arange(ATT_WINDOW)[None, :]
    band = kj - (qi // CHUNK) * CHUNK
    in_band = (band >= 0) & (band < ATT_BAND)
    return jnp.where(in_band[None], toeplitz[:, :, :ATT_WINDOW], NEG_INF)


def _pad_rows_to8(a, axis):
    pad = [(0, 0)] * a.ndim
    pad[axis] = (SUBLANES - a.shape[axis], 0)
    return jnp.pad(a, pad)


def _layer(x, lw, layer, *, nseq, seq_len, tm, ssd_chunk, ssm_conv_prefix, ssm_h0, sc_prefix, kv_cache):
    xn = _rmsnorm(x, lw["norm_mix_g"], tm)
    tn = 1024
    wt = lw["w_in_t"]
    off = IN_OFFSETS
    tiles_per_seq = max(1, seq_len // tm)
    tail_rows = min(tm, min(ATT_PAST, seq_len))
    if tm > seq_len:
        tail_rows = tm

    (zs,) = _proj_act(xn, wt, layer, off[0], SSM_INNER, tm=tm, tn=tn, act=lambda a: a * jax.nn.sigmoid(a),
                      out_kinds=["f32"], name="proj_z")
    (dt,) = _proj_act(xn, wt, layer, off[2], LANES, tm=tm, tn=LANES, act=jax.nn.softplus, out_kinds=["f32"],
                      bias=lw["dt_bias128"], valid_rows=SSM_HEADS, name="proj_dt")
    (gates,) = _proj_act(xn, wt, layer, off[9], N_BRANCH * D_MODEL, tm=tm, tn=tn, act=jax.nn.sigmoid,
                         out_kinds=["f32"], name="proj_gates")
    v16, v_tail = _proj_act(xn, wt, layer, off[8], ATT_DIM, tm=tm, tn=tn, act=lambda a: a,
                            out_kinds=["bf16", "tail"], tail_rows=tail_rows, tiles_per_seq=tiles_per_seq,
                            name="proj_v")
    (q16,) = _proj_headnorm(xn, wt, layer, off[6], ATT_DIM, lw["q_norm_g"], tm=tm, tn=tn,
                            scale=HEAD_DIM ** -0.5, out_kinds=["bf16"], name="proj_q")
    k16, k_tail = _proj_headnorm(xn, wt, layer, off[7], ATT_DIM, lw["k_norm_g"], tm=tm, tn=tn, scale=1.0,
                                 out_kinds=["bf16", "tail"], tail_rows=tail_rows, tiles_per_seq=tiles_per_seq,
                                 name="proj_k")
    xbc_act, ssm_conv_state = _proj_conv(_proj_xbc_kernel, xn, wt, layer, [off[1]], SSM_CONV_DIM,
                                         lw["ssm_conv_w8"], lw["ssm_conv_b"], ssm_conv_prefix, tm=tm, tn=512,
                                         seq_len=seq_len, out_dtype=F32, name="proj_xbc")
    y_sc, sc_state = _proj_conv(_proj_sc_kernel, xn, wt, layer, [off[3], off[4], off[5]], SC_DIM,
                                lw["sc_conv_w8"], None, sc_prefix, tm=tm, tn=512, seq_len=seq_len,
                                out_dtype=BF16, name="proj_sc")

    els = lw["ech"] if ssd_chunk == SSM_HEAD_DIM else _head_expand(ssd_chunk)
    y_ssm, h_last = _ssd(xbc_act, dt, zs, lw["a_log128"], lw["d_x"], lw["ssm_norm_g"], lw["ech"], els, ssm_h0,
                         L=ssd_chunk, seq_len=seq_len)

    if kv_cache is None:
        o = _attn_prompt(q16, k16, v16, lw["prompt_bias"], nseq=nseq, seq_len=seq_len)
    else:
        o = _attn_sample(q16, k16, v16, kv_cache[0], kv_cache[1], layer, lw["sample_bias"], nseq=nseq,
                         seq_len=seq_len)

    mtm = min(tm, 512)
    merged = _merge(y_ssm, y_sc, o, lw["ssm_out_w"], lw["sc_out_w"], lw["attn_out_w"], layer, gates,
                    tm=mtm, tn=512)
    h, hn = _wo(x, merged, lw["w_o"], lw["norm_ffn_g"], tm=mtm)
    y = _ffn(h, hn, lw["ffn_w1"], lw["ffn_w2"], tm=mtm, tc=1024)
    return y, (k_tail, v_tail, h_last, ssm_conv_state, sc_state)


def kernel(x_prompt, x_sample, cache_attn_k, cache_attn_v, state_ssm, state_ssm_conv, state_short_conv,
           norm_mix_g, w_in, ssm_conv_w, ssm_conv_b, ssm_dt_bias, ssm_a_log, ssm_d, ssm_norm_g, ssm_out_w,
           sc_conv_w, sc_out_w, q_norm_g, k_norm_g, rel_bias, attn_out_w, w_o, norm_ffn_g, ffn_w1, ffn_w2):
    bp, lp, d = x_prompt.shape
    bs, ls, _ = x_sample.shape
    lc = cache_attn_k.shape[2]
    assert lc == ATT_PAST and lc + ls <= TOEPLITZ_COLS and ls <= ATT_Q_BLOCK
    ech = _head_expand(SSM_HEAD_DIM)
    w_in_t = jnp.swapaxes(w_in, 1, 2)
    k_cache16 = cache_attn_k.astype(BF16).reshape(DEPTH, bs, lc, ATT_DIM)
    v_cache16 = cache_attn_v.astype(BF16).reshape(DEPTH, bs, lc, ATT_DIM)

    yp = x_prompt.reshape(bp * lp, d)
    ys = x_sample.reshape(bs * ls, d)
    new_p, new_s = [], []
    for l in range(DEPTH):
        toeplitz = _toeplitz_bias(rel_bias[l])
        lw = {
            "norm_mix_g": norm_mix_g[l], "norm_ffn_g": norm_ffn_g[l],
            "w_in_t": w_in_t,
            "dt_bias128": jnp.pad(ssm_dt_bias[l].astype(F32), (0, LANES - SSM_HEADS)),
            "a_log128": jnp.pad(ssm_a_log[l].astype(F32), (0, LANES - SSM_HEADS)).reshape(1, LANES),
            "d_x": jnp.repeat(ssm_d[l].astype(F32), SSM_HEAD_DIM).reshape(1, SSM_INNER),
            "ssm_norm_g": ssm_norm_g[l].astype(F32).reshape(1, SSM_INNER),
            "ssm_conv_w8": jnp.pad(ssm_conv_w[l].astype(F32), ((0, SUBLANES - SSM_CONV), (0, 0))),
            "ssm_conv_b": ssm_conv_b[l].astype(F32),
            "sc_conv_w8": jnp.pad(sc_conv_w[l].astype(F32), ((0, SUBLANES - SC_WIDTH), (0, 0))),
            "q_norm_g": q_norm_g[l].astype(F32), "k_norm_g": k_norm_g[l].astype(F32),
            "ssm_out_w": ssm_out_w, "sc_out_w": sc_out_w, "attn_out_w": attn_out_w,
            "w_o": _cast_bf16(w_o, l, tr=1024),
            "ffn_w1": _cast_bf16(ffn_w1, l, tr=256), "ffn_w2": _cast_bf16(ffn_w2, l, tr=1024),
            "ech": ech,
            "prompt_bias": _prompt_bias(toeplitz),
            "sample_bias": toeplitz[:, :ls, :lc + ls],
        }
        yp, st_p = _layer(
            yp, lw, l, nseq=bp, seq_len=lp, tm=1024, ssd_chunk=CHUNK,
            ssm_conv_prefix=jnp.zeros((bp, SUBLANES, SSM_CONV_DIM), F32),
            ssm_h0=jnp.zeros((bp, SSM_INNER, SSM_STATE), F32),
            sc_prefix=jnp.zeros((bp, SUBLANES, SC_DIM), F32), kv_cache=None)
        ys, st_s = _layer(
            ys, lw, l, nseq=bs, seq_len=ls, tm=bs * ls, ssd_chunk=ls,
            ssm_conv_prefix=_pad_rows_to8(state_ssm_conv[l].astype(F32), 1),
            ssm_h0=state_ssm[l].astype(F32).reshape(bs, SSM_INNER, SSM_STATE),
            sc_prefix=_pad_rows_to8(state_short_conv[l].astype(F32), 1),
            kv_cache=(k_cache16, v_cache16))
        new_p.append(st_p)
        new_s.append(st_s)

    keep = min(ATT_PAST, lp)

    def stack(states, fn):
        return jnp.stack([fn(s) for s in states])

    hshape = lambda b: (b, SSM_HEADS, SSM_HEAD_DIM, SSM_STATE)
    return (
        yp.reshape(bp, lp, d),
        ys.reshape(bs, ls, d),
        stack(new_p, lambda s: s[0].reshape(bp, keep, N_HEADS, HEAD_DIM)),
        stack(new_p, lambda s: s[1].reshape(bp, keep, N_HEADS, HEAD_DIM)),
        stack(new_s, lambda s: s[0].reshape(bs, ls, N_HEADS, HEAD_DIM)),
        stack(new_s, lambda s: s[1].reshape(bs, ls, N_HEADS, HEAD_DIM)),
        stack(new_p, lambda s: s[2].reshape(hshape(bp))),
        stack(new_s, lambda s: s[2].reshape(hshape(bs))),
        stack(new_p, lambda s: s[3][:, SUBLANES - (SSM_CONV - 1):]),
        stack(new_s, lambda s: s[3][:, SUBLANES - (SSM_CONV - 1):]),
        stack(new_p, lambda s: s[4][:, SUBLANES - (SC_WIDTH - 1):]),
        stack(new_s, lambda s: s[4][:, SUBLANES - (SC_WIDTH - 1):]),
    )
```

```python
import functools
import math

import jax
import jax.numpy as jnp
from jax import lax
from jax.experimental import pallas as pl
from jax.experimental.pallas import tpu as pltpu

F32 = jnp.float32
BF16 = jnp.bfloat16

D_MODEL = 2048
DEPTH = 2
CHUNK = 64
EPS = 1e-6

SSM_INNER = D_MODEL
SSM_HEAD_DIM = 64
SSM_HEADS = SSM_INNER // SSM_HEAD_DIM
SSM_GROUPS = 4
SSM_STATE = 128
SSM_CONV = 4
SSM_BC = SSM_GROUPS * SSM_STATE
SSM_CONV_DIM = SSM_INNER + 2 * SSM_BC
SSM_GROUP_DIM = SSM_INNER // SSM_GROUPS

SC_DIM = D_MODEL
SC_WIDTH = 3

N_HEADS = 16
HEAD_DIM = D_MODEL // N_HEADS
ATT_DIM = N_HEADS * HEAD_DIM
ATT_PAST_CHUNKS = 8
ATT_PAST = ATT_PAST_CHUNKS * CHUNK
ATT_BAND = (ATT_PAST_CHUNKS + 1) * CHUNK
MAX_REL = 128

N_BRANCH = 3
D_FF = 4 * D_MODEL

IN_SPLITS = (SSM_INNER, SSM_CONV_DIM, SSM_HEADS, SC_DIM, SC_DIM, SC_DIM, ATT_DIM, ATT_DIM, ATT_DIM,
             N_BRANCH * D_MODEL)
IN_OFFSETS = tuple(int(sum(IN_SPLITS[:i])) for i in range(len(IN_SPLITS) + 1))

NEG_INF = -1e30

LANES = 128
SUBLANES = 8
MIB = 1024 * 1024

ATT_Q_CHUNKS = 4
ATT_Q_BLOCK = ATT_Q_CHUNKS * CHUNK
ATT_K_BLOCKS = (ATT_PAST_CHUNKS + ATT_Q_CHUNKS) // ATT_Q_CHUNKS
ATT_WINDOW = ATT_K_BLOCKS * ATT_Q_BLOCK
TOEPLITZ_COLS = 1024
SAMPLE_KEY_TILE = 2048
SSD_CHUNKS_PER_STEP = 4
PROMPT_ROW_TILE = 1024


def _cparams(semantics, vmem_mib):
    return pltpu.CompilerParams(dimension_semantics=semantics, vmem_limit_bytes=vmem_mib * MIB)


def _dot(a, b):
    return jnp.dot(a, b, preferred_element_type=F32)


def _dot_nt(a, b):
    return lax.dot_general(a, b, (((1,), (1,)), ((), ())), preferred_element_type=F32)


def _split_bf16(v, parts):
    out = []
    r = v
    for _ in range(parts):
        p = r.astype(BF16)
        out.append(p)
        r = r - p.astype(F32)
    return out


def _dot_exact_lhs(vs, m, parts):
    rows = vs[0].shape[0]
    terms = [p for v, n in zip(vs, parts) for p in _split_bf16(v, n)]
    prod = _dot(jnp.concatenate(terms, axis=0), m)
    outs, at = [], 0
    for n in parts:
        acc = prod[at * rows:(at + 1) * rows]
        for j in range(1, n):
            acc = acc + prod[(at + j) * rows:(at + j + 1) * rows]
        outs.append(acc)
        at += n
    return outs


def _dot_exact_rhs(m, v, parts):
    cols = v.shape[1]
    prod = _dot(m, jnp.concatenate(_split_bf16(v, parts), axis=1))
    acc = prod[:, 0:cols]
    for j in range(1, parts):
        acc = acc + prod[:, j * cols:(j + 1) * cols]
    return acc


def _two_group_kernel(body, pre, *, n_shared, n_in, n_out, row_axis, n_prompt_tiles):
    def kernel(*refs):
        shared = refs[:n_shared]
        in_p = refs[n_shared:n_shared + n_in]
        in_s = refs[n_shared + n_in:n_shared + 2 * n_in]
        o0 = n_shared + 2 * n_in
        out_p = refs[o0:o0 + n_out]
        out_s = refs[o0 + n_out:o0 + 2 * n_out]
        scratch = refs[o0 + 2 * n_out:]
        if pre is not None:
            pre(shared, scratch)
        i = pl.program_id(row_axis)

        @pl.when(i < n_prompt_tiles)
        def _():
            body(shared, in_p, out_p, scratch, 0)

        @pl.when(i == n_prompt_tiles)
        def _():
            body(shared, in_s, out_s, scratch, 1)

    return kernel


def _two_group_call(body, pre, *, grid, row_axis, shared, ins_p, ins_s, outs_p, outs_s, scratch_shapes, vmem_mib,
                    name):
    assert len(ins_p) == len(ins_s) and len(outs_p) == len(outs_s)
    arrays = [a for a, _ in shared + ins_p + ins_s]
    in_specs = [s for _, s in shared + ins_p + ins_s]
    res = pl.pallas_call(
        _two_group_kernel(body, pre, n_shared=len(shared), n_in=len(ins_p), n_out=len(outs_p), row_axis=row_axis,
                          n_prompt_tiles=grid[row_axis] - 1),
        grid=grid,
        in_specs=in_specs,
        out_specs=[s for _, s in outs_p + outs_s],
        out_shape=[a for a, _ in outs_p + outs_s],
        scratch_shapes=scratch_shapes,
        compiler_params=_cparams(("arbitrary",) * len(grid), vmem_mib),
        name=name,
    )(*arrays)
    return res[:len(outs_p)], res[len(outs_p):]


def _row_specs(tm, ts, n_prompt_tiles, cols, *, col_of=None, two_d_grid=True):
    last = n_prompt_tiles - 1
    if not two_d_grid:
        return (pl.BlockSpec((tm, cols), lambda i, *_: (jnp.minimum(i, last), 0)),
                pl.BlockSpec((ts, cols), lambda i, *_: (0, 0)))
    col_of = col_of or (lambda j: j)
    return (pl.BlockSpec((tm, cols), lambda j, i: (jnp.minimum(i, last), col_of(j))),
            pl.BlockSpec((ts, cols), lambda j, i: (0, col_of(j))))


def _rmsnorm_body(shared, ins, outs, scratch, group):
    x = ins[0][...]
    ms = jnp.mean(x * x, axis=-1, keepdims=True)
    outs[0][...] = (x * lax.rsqrt(ms + EPS) * shared[0][...]).astype(outs[0].dtype)


def _rmsnorm(xp, xs, g, tm):
    d = xp.shape[1]
    npt, ts = xp.shape[0] // tm, xs.shape[0]
    sp, ss = _row_specs(tm, ts, npt, d, two_d_grid=False)
    (op,), (os_,) = _two_group_call(
        _rmsnorm_body, None, grid=(npt + 1,), row_axis=0,
        shared=[(g.reshape(1, d), pl.BlockSpec((1, d), lambda i: (0, 0)))],
        ins_p=[(xp, sp)], ins_s=[(xs, ss)],
        outs_p=[(jax.ShapeDtypeStruct(xp.shape, BF16), sp)], outs_s=[(jax.ShapeDtypeStruct(xs.shape, BF16), ss)],
        scratch_shapes=[], vmem_mib=40, name="rmsnorm")
    return op, os_


def _cast_kernel(w_ref, o_ref):
    o_ref[...] = w_ref[0].astype(o_ref.dtype)


def _cast_bf16(w, layer, *, tr):
    _, r, c = w.shape
    return pl.pallas_call(
        _cast_kernel,
        grid=(r // tr,),
        in_specs=[pl.BlockSpec((1, tr, c), lambda i: (layer, i, 0))],
        out_specs=pl.BlockSpec((tr, c), lambda i: (i, 0)),
        out_shape=jax.ShapeDtypeStruct((r, c), BF16),
        compiler_params=_cparams(("arbitrary",), 40),
        name="cast_bf16",
    )(w)


def _wspec(k, tn, layer, row0):
    assert row0 % SUBLANES == 0 and tn % SUBLANES == 0
    return pl.BlockSpec((pl.Element(1), pl.Element(tn), pl.Element(k)),
                        lambda j, i: (layer, pl.multiple_of(row0 + j * tn, SUBLANES), 0))


def _load_weight(w_ref, w16_ref, slot=None, valid_rows=None):
    @pl.when(pl.program_id(1) == 0)
    def _():
        w = w_ref[0]
        if valid_rows is not None:
            rows = lax.broadcasted_iota(jnp.int32, w.shape, 0)
            w = jnp.where(rows < valid_rows, w, 0.0)
        if slot is None:
            w16_ref[...] = w.astype(BF16)
        else:
            w16_ref[slot] = w.astype(BF16)


def _proj_outs(out_kinds, tp, ts, n, tm, tn, npt, tail_rows, tiles_per_seq):
    last = npt - 1
    outs_p, outs_s = [], []
    for kind in out_kinds:
        sp, ss = _row_specs(tm, ts, npt, tn)
        if kind == "tail":
            nseq = tp // (tm * tiles_per_seq)
            outs_p.append((jax.ShapeDtypeStruct((nseq * tail_rows, n), F32),
                           pl.BlockSpec((tail_rows, tn), lambda j, i: (jnp.minimum(i, last) // tiles_per_seq, j))))
            outs_s.append((jax.ShapeDtypeStruct((ts, n), F32), ss))
        else:
            dt = F32 if kind == "f32" else BF16
            outs_p.append((jax.ShapeDtypeStruct((tp, n), dt), sp))
            outs_s.append((jax.ShapeDtypeStruct((ts, n), dt), ss))
    return outs_p, outs_s


def _store_proj(y, out_kinds, outs, cols=slice(None), scale=1.0):
    for kind, o_ref in zip(out_kinds, outs):
        if kind == "tail":
            o_ref[:, cols] = y[y.shape[0] - o_ref.shape[0]:]
        elif kind == "bf16":
            o_ref[:, cols] = (y * scale).astype(BF16) if scale != 1.0 else y.astype(BF16)
        else:
            o_ref[:, cols] = y


def _proj_act(xnp, xns, wt, layer, row0, n, *, tm, tn, act, out_kinds, bias=None, valid_rows=None, tail_rows=None,
              tiles_per_seq=1, name):
    (tp, k), ts = xnp.shape, xns.shape[0]
    npt = tp // tm
    has_bias = bias is not None

    def pre(shared, scratch):
        _load_weight(shared[0], scratch[0], valid_rows=valid_rows)

    def body(shared, ins, outs, scratch, group):
        acc = _dot_nt(ins[0][...], scratch[0][...])
        if has_bias:
            acc = acc + shared[1][...]
        _store_proj(act(acc), out_kinds, outs)

    shared = [(wt, _wspec(k, tn, layer, row0))]
    if has_bias:
        shared.append((bias.reshape(1, n), pl.BlockSpec((1, tn), lambda j, i: (0, j))))
    xp_spec, xs_spec = _row_specs(tm, ts, npt, k, col_of=lambda j: 0)
    outs_p, outs_s = _proj_outs(out_kinds, tp, ts, n, tm, tn, npt, tail_rows, tiles_per_seq)
    return _two_group_call(body, pre, grid=(n // tn, npt + 1), row_axis=1, shared=shared,
                           ins_p=[(xnp, xp_spec)], ins_s=[(xns, xs_spec)], outs_p=outs_p, outs_s=outs_s,
                           scratch_shapes=[pltpu.VMEM((tn, k), BF16)], vmem_mib=56, name=name)


def _proj_headnorm(xnp, xns, wt, layer, row0, n, g, *, tm, tn, scale, out_kinds, tail_rows=None, tiles_per_seq=1,
                   name):
    (tp, k), ts = xnp.shape, xns.shape[0]
    npt = tp // tm

    def pre(shared, scratch):
        _load_weight(shared[0], scratch[0])

    def body(shared, ins, outs, scratch, group):
        acc = _dot_nt(ins[0][...], scratch[0][...])
        gain = shared[1][...]
        for h in range(tn // HEAD_DIM):
            sl = slice(h * HEAD_DIM, (h + 1) * HEAD_DIM)
            blk = acc[:, sl]
            ms = jnp.mean(blk * blk, axis=-1, keepdims=True)
            _store_proj(blk * lax.rsqrt(ms + EPS) * gain, out_kinds, outs, cols=sl, scale=scale)

    shared = [(wt, _wspec(k, tn, layer, row0)),
              (g.reshape(1, HEAD_DIM), pl.BlockSpec((1, HEAD_DIM), lambda j, i: (0, 0)))]
    xp_spec, xs_spec = _row_specs(tm, ts, npt, k, col_of=lambda j: 0)
    outs_p, outs_s = _proj_outs(out_kinds, tp, ts, n, tm, tn, npt, tail_rows, tiles_per_seq)
    return _two_group_call(body, pre, grid=(n // tn, npt + 1), row_axis=1, shared=shared,
                           ins_p=[(xnp, xp_spec)], ins_s=[(xns, xs_spec)], outs_p=outs_p, outs_s=outs_s,
                           scratch_shapes=[pltpu.VMEM((tn, k), BF16)], vmem_mib=56, name=name)


CONV_SUB = 256


def _causal_conv(u, cs, cw_ref, p_ref, carry_ref, st_ref, *, width, nseg, tiles_per_seq):
    tm, tn = u.shape
    seg_len = tm // nseg
    row8 = lax.broadcasted_iota(jnp.int32, (SUBLANES, tn), 0)
    outs = []
    for s in range(nseg):
        seg = u[s * seg_len:(s + 1) * seg_len]
        prev8 = carry_ref[:, cs] if tiles_per_seq > 1 else p_ref[s, :, cs]
        acc = cw_ref[width - 1:width, cs] * seg
        for k in range(1, width):
            sh = pltpu.roll(seg, k, 0)
            first8 = jnp.where(row8 < k, pltpu.roll(prev8, k, 0), sh[0:SUBLANES])
            shk = jnp.concatenate([first8, sh[SUBLANES:]], axis=0)
            acc = acc + cw_ref[width - 1 - k:width - k, cs] * shk
        outs.append(acc)
        st_ref[s, :, cs] = seg[seg_len - SUBLANES:seg_len]
    if tiles_per_seq > 1:
        carry_ref[:, cs] = u[tm - SUBLANES:tm]
    return outs[0] if nseg == 1 else jnp.concatenate(outs, axis=0)


def _proj_conv(kind, xnp, xns, wt, layer, row0s, n, conv_w8, conv_b, prefix_p, prefix_s, *, tm, tn, seq_p, seq_s,
               out_dtype, name):
    (tp, k), ts = xnp.shape, xns.shape[0]
    npt = tp // tm
    last = npt - 1
    tps_p = seq_p // tm
    nseg_s = ts // seq_s
    assert tps_p >= 1 and seq_p % tm == 0 and ts % seq_s == 0
    width = SSM_CONV if kind == "xbc" else SC_WIDTH
    nw = len(row0s)
    has_b = conv_b is not None

    def pre(shared, scratch):
        for slot in range(nw):
            _load_weight(shared[slot], scratch[1], slot=slot)

    def body(shared, ins, outs, scratch, group):
        x_ref, p_ref = ins
        o_ref, st_ref = outs
        carry_ref, w16_ref = scratch
        cw_ref = shared[nw]
        nseg, tps = (1, tps_p) if group == 0 else (nseg_s, 1)
        if tps > 1:
            @pl.when(lax.rem(pl.program_id(1), tps) == 0)
            def _():
                carry_ref[...] = p_ref[0]
        x = x_ref[...]
        for c in range(tn // CONV_SUB):
            cs = slice(c * CONV_SUB, (c + 1) * CONV_SUB)
            if kind == "xbc":
                u = _dot_nt(x, w16_ref[0, cs, :])
            else:
                u = _dot_nt(x, w16_ref[1, cs, :]) * _dot_nt(x, w16_ref[2, cs, :])
            y = _causal_conv(u, cs, cw_ref, p_ref, carry_ref, st_ref, width=width, nseg=nseg, tiles_per_seq=tps)
            if kind == "xbc":
                y = y + shared[nw + 1][:, cs]
                o_ref[:, cs] = y * jax.nn.sigmoid(y)
            else:
                o_ref[:, cs] = (_dot_nt(x, w16_ref[0, cs, :]) * y).astype(o_ref.dtype)

    shared = [(wt, _wspec(k, tn, layer, r)) for r in row0s]
    shared.append((conv_w8, pl.BlockSpec((SUBLANES, tn), lambda j, i: (0, j))))
    if has_b:
        shared.append((conv_b.reshape(1, n), pl.BlockSpec((1, tn), lambda j, i: (0, j))))
    xp_spec, xs_spec = _row_specs(tm, ts, npt, k, col_of=lambda j: 0)
    yp_spec, ys_spec = _row_specs(tm, ts, npt, tn)
    ins_p = [(xnp, xp_spec),
             (prefix_p, pl.BlockSpec((1, SUBLANES, tn), lambda j, i: (jnp.minimum(i, last) // tps_p, 0, j)))]
    ins_s = [(xns, xs_spec), (prefix_s, pl.BlockSpec((nseg_s, SUBLANES, tn), lambda j, i: (0, 0, j)))]
    outs_p = [(jax.ShapeDtypeStruct((tp, n), out_dtype), yp_spec),
              (jax.ShapeDtypeStruct((npt, SUBLANES, n), F32),
               pl.BlockSpec((1, SUBLANES, tn), lambda j, i: (jnp.minimum(i, last), 0, j)))]
    outs_s = [(jax.ShapeDtypeStruct((ts, n), out_dtype), ys_spec),
              (jax.ShapeDtypeStruct((nseg_s, SUBLANES, n), F32),
               pl.BlockSpec((nseg_s, SUBLANES, tn), lambda j, i: (0, 0, j)))]
    (yp, tails_p), (ys, tails_s) = _two_group_call(
        body, pre, grid=(n // tn, npt + 1), row_axis=1, shared=shared, ins_p=ins_p, ins_s=ins_s, outs_p=outs_p,
        outs_s=outs_s, scratch_shapes=[pltpu.VMEM((SUBLANES, tn), F32), pltpu.VMEM((nw, tn, k), BF16)],
        vmem_mib=56, name=name)
    return (yp, tails_p[tps_p - 1::tps_p]), (ys, tails_s)


def _ssd_kernel(x_ref, b_ref, c_ref, dt_ref, zs_ref, alog_ref, dx_ref, ng_ref, ech_ref, els_ref, h0_ref,
                y_ref, hl_ref, ht_ref, *, L, cps, nsteps):
    ci = pl.program_id(1)
    hp = LANES // L
    ntiles = SSM_HEADS // hp
    gw = SSM_GROUP_DIM
    tw = hp * SSM_HEAD_DIM
    log2_l = L.bit_length() - 1

    @pl.when(ci == 0)
    def _():
        ht_ref[...] = h0_ref[0].T

    a = -jnp.exp(alog_ref[...])
    ri = lax.broadcasted_iota(jnp.int32, (L, L), 0)
    cj = lax.broadcasted_iota(jnp.int32, (L, L), 1)
    tri = (ri >= cj).astype(BF16)
    lane_blk = jnp.right_shift(lax.broadcasted_iota(jnp.int32, (1, LANES), 1), log2_l)
    row_l = lax.broadcasted_iota(jnp.int32, (L, LANES), 0)
    lane_s = jnp.bitwise_and(lax.broadcasted_iota(jnp.int32, (L, LANES), 1), L - 1)
    causal = row_l >= lane_s
    rb = jnp.right_shift(lax.broadcasted_iota(jnp.int32, (LANES, tw), 0), log2_l)
    cb_ = jnp.right_shift(lax.broadcasted_iota(jnp.int32, (LANES, tw), 1), SSM_HEAD_DIM.bit_length() - 1)
    blockdiag = rb == cb_
    tile_rows = lambda v: jnp.concatenate([v] * hp, axis=0)
    zpad = jnp.zeros((LANES - L, gw), F32)

    for r in range(cps):
        rows = slice(r * L, (r + 1) * L)
        x = x_ref[rows, :]
        bm = b_ref[rows, :]
        cm = c_ref[rows, :]
        dt = dt_ref[rows, :]
        da = dt * a

        acum = _dot_exact_rhs(tri, da, 3)
        eacum = jnp.exp(acum)
        dend = jnp.exp(acum[L - 1:L, :] - acum)
        w = dt * dend

        ech = ech_ref[...]
        if L == SSM_HEAD_DIM:
            cexp = jnp.concatenate(
                [jnp.broadcast_to(acum[:, h:h + 1], (L, SSM_HEAD_DIM)) for h in range(SSM_HEADS)], axis=1)
        else:
            (cexp,) = _dot_exact_lhs([acum], els_ref[...], [3])
        wx, ex = _dot_exact_lhs([w, eacum], ech, [2, 2])

        acum_t = tile_rows(acum).T
        dt_t = tile_rows(dt).T

        bsq = [tile_rows(bm[:, g * SSM_STATE:(g + 1) * SSM_STATE]) for g in range(SSM_GROUPS)]
        cbt = [_dot_nt(cm[:, g * SSM_STATE:(g + 1) * SSM_STATE].astype(BF16), bsq[g].astype(BF16))
               for g in range(SSM_GROUPS)]

        yd = []
        for t in range(ntiles):
            h_first = t * hp
            g = h_first // (SSM_HEADS // SSM_GROUPS)
            r_row = acum_t[h_first:h_first + 1, :]
            d_row = dt_t[h_first:h_first + 1, :]
            for jj in range(1, hp):
                sel = lane_blk == jj
                r_row = jnp.where(sel, acum_t[h_first + jj:h_first + jj + 1, :], r_row)
                d_row = jnp.where(sel, dt_t[h_first + jj:h_first + jj + 1, :], d_row)
            diff = cexp[:, t * LANES:(t + 1) * LANES] - r_row
            dec = jnp.exp(jnp.where(causal, diff, -jnp.inf))
            sc = (cbt[g] * dec * d_row).astype(BF16)
            xs = tile_rows(x[:, t * tw:(t + 1) * tw])
            rhs = jnp.where(blockdiag, xs, 0.0).astype(BF16)
            yd.append(_dot(sc, rhs))
        y = jnp.concatenate(yd, axis=1)

        xw = x * wx
        for g in range(SSM_GROUPS):
            gs = slice(g * gw, (g + 1) * gw)
            h_in = ht_ref[:, gs]
            y_off = _dot(cm[:, g * SSM_STATE:(g + 1) * SSM_STATE].astype(BF16), h_in.astype(BF16))
            yg = y[:, gs] + y_off * ex[:, gs] + dx_ref[:, gs] * x[:, gs]
            yg = yg * zs_ref[rows, gs]
            ms = jnp.mean(yg * yg, axis=-1, keepdims=True)
            y_ref[rows, gs] = (yg * lax.rsqrt(ms + EPS) * ng_ref[:, gs]).astype(y_ref.dtype)
            bm_t = bsq[g].T.astype(BF16)
            upd = jnp.concatenate([xw[:, gs], zpad], axis=0).astype(BF16)
            ht_ref[:, gs] = h_in * ex[L - 1:L, gs] + _dot(bm_t, upd)

    @pl.when(ci == nsteps - 1)
    def _():
        hl_ref[0] = ht_ref[...].T


def _ssd(xbc_act, dt, zs, a_log128, d_x, norm_g, ech, els, h0, *, L, cps, seq_len):
    t = xbc_act.shape[0]
    nseq = h0.shape[0]
    rows = L * cps
    ns = seq_len // rows
    rmap = lambda b, c: (b * ns + c, 0)
    cmap = lambda b, c: (0, 0)
    nb = SSM_INNER // SSM_BC
    return pl.pallas_call(
        functools.partial(_ssd_kernel, L=L, cps=cps, nsteps=ns),
        grid=(nseq, ns),
        in_specs=[pl.BlockSpec((rows, SSM_INNER), rmap),
                  pl.BlockSpec((rows, SSM_BC), lambda b, c: (b * ns + c, nb)),
                  pl.BlockSpec((rows, SSM_BC), lambda b, c: (b * ns + c, nb + 1)),
                  pl.BlockSpec((rows, LANES), rmap),
                  pl.BlockSpec((rows, SSM_INNER), rmap),
                  pl.BlockSpec((1, LANES), cmap),
                  pl.BlockSpec((1, SSM_INNER), cmap),
                  pl.BlockSpec((1, SSM_INNER), cmap),
                  pl.BlockSpec(ech.shape, cmap),
                  pl.BlockSpec(els.shape, cmap),
                  pl.BlockSpec((1, SSM_INNER, SSM_STATE), lambda b, c: (b, 0, 0))],
        out_specs=[pl.BlockSpec((rows, SSM_INNER), rmap),
                   pl.BlockSpec((1, SSM_INNER, SSM_STATE), lambda b, c: (b, 0, 0))],
        out_shape=[jax.ShapeDtypeStruct((t, SSM_INNER), BF16),
                   jax.ShapeDtypeStruct((nseq, SSM_INNER, SSM_STATE), F32)],
        scratch_shapes=[pltpu.VMEM((SSM_STATE, SSM_INNER), F32)],
        compiler_params=_cparams(("arbitrary", "arbitrary"), 48),
        name=f"ssd_L{L}",
    )(xbc_act, xbc_act, xbc_act, dt, zs, a_log128, d_x, norm_g, ech, els, h0)


def _softmax_pv(s_parts, v_parts):
    m = functools.reduce(jnp.maximum, [jnp.max(s, axis=1, keepdims=True) for s in s_parts])
    l = None
    o = None
    for s, v in zip(s_parts, v_parts):
        p = jnp.exp(s - m)
        ls = jnp.sum(p, axis=1, keepdims=True)
        os_ = _dot(p.astype(BF16), v)
        l = ls if l is None else l + ls
        o = os_ if o is None else o + os_
    return o / l


def _attn_prompt_kernel(q_ref, k0_ref, k1_ref, k2_ref, v0_ref, v1_ref, v2_ref, bias_ref, o_ref):
    i = pl.program_id(1)
    k_refs = (k0_ref, k1_ref, k2_ref)
    v_refs = (v0_ref, v1_ref, v2_ref)
    first_valid = (ATT_K_BLOCKS - 1 - i) * ATT_Q_BLOCK
    kidx = lax.broadcasted_iota(jnp.int32, (ATT_Q_BLOCK, ATT_Q_BLOCK), 1)
    for h in range(N_HEADS):
        sl = slice(h * HEAD_DIM, (h + 1) * HEAD_DIM)
        q = q_ref[:, sl]
        s_parts = []
        for kb in range(ATT_K_BLOCKS):
            s = _dot_nt(q, k_refs[kb][:, sl]) + bias_ref[h, :, kb * ATT_Q_BLOCK:(kb + 1) * ATT_Q_BLOCK]
            s_parts.append(jnp.where(kidx + kb * ATT_Q_BLOCK >= first_valid, s, NEG_INF))
        o = _softmax_pv(s_parts, [v_refs[kb][:, sl] for kb in range(ATT_K_BLOCKS)])
        o_ref[:, sl] = o.astype(o_ref.dtype)


def _attn_prompt(q, k, v, bias, *, nseq, seq_len):
    t = q.shape[0]
    nqb = seq_len // ATT_Q_BLOCK
    qmap = lambda b, i: (b * nqb + i, 0)

    def kmap(back):
        return lambda b, i: (b * nqb + jnp.maximum(i - back, 0), 0)

    blk = (ATT_Q_BLOCK, ATT_DIM)
    kv_specs = [pl.BlockSpec(blk, kmap(ATT_K_BLOCKS - 1 - kb)) for kb in range(ATT_K_BLOCKS)]
    return pl.pallas_call(
        _attn_prompt_kernel,
        grid=(nseq, nqb),
        in_specs=[pl.BlockSpec(blk, qmap)] + kv_specs + kv_specs
        + [pl.BlockSpec(bias.shape, lambda b, i: (0, 0, 0))],
        out_specs=pl.BlockSpec(blk, qmap),
        out_shape=jax.ShapeDtypeStruct((t, ATT_DIM), BF16),
        compiler_params=_cparams(("arbitrary", "arbitrary"), 56),
        name="attn_prompt",
    )(q, k, k, k, v, v, v, bias)


def _heads_to_rows(ref):
    return jnp.concatenate([ref[:, h * HEAD_DIM:(h + 1) * HEAD_DIM] for h in range(N_HEADS)], axis=0)


def _attn_sample_kernel(q_ref, kn_ref, vn_ref, kc_ref, vc_ref, bc_ref, bn_ref, o_ref):
    t = q_ref.shape[0]
    q2 = _heads_to_rows(q_ref)
    kn2, vn2 = _heads_to_rows(kn_ref), _heads_to_rows(vn_ref)
    rows = kc_ref.shape[2]
    s_parts, v_parts = [], []
    for c in range(rows // SAMPLE_KEY_TILE):
        ks = slice(c * SAMPLE_KEY_TILE, (c + 1) * SAMPLE_KEY_TILE)
        s_parts.append(_dot_nt(q2, kc_ref[0, 0, ks, :].astype(BF16)) + bc_ref[:, ks])
        v_parts.append(vc_ref[0, 0, ks, :].astype(BF16))
    s_parts.append(_dot_nt(q2, kn2) + bn_ref[...])
    v_parts.append(vn2)
    o2 = _softmax_pv(s_parts, v_parts)
    for h in range(N_HEADS):
        o_ref[:, h * HEAD_DIM:(h + 1) * HEAD_DIM] = o2[h * t:(h + 1) * t].astype(o_ref.dtype)


def _attn_sample(q, k, v, k_cache, v_cache, layer, bias_c, bias_n, *, nseq, seq_len):
    t = q.shape[0]
    rows = k_cache.shape[2]
    blk = (seq_len, ATT_DIM)
    rmap = lambda b: (b, 0)
    cspec = pl.BlockSpec((1, 1, rows, HEAD_DIM), lambda b: (layer, b, 0, 0))
    return pl.pallas_call(
        _attn_sample_kernel,
        grid=(nseq,),
        in_specs=[pl.BlockSpec(blk, rmap), pl.BlockSpec(blk, rmap), pl.BlockSpec(blk, rmap), cspec, cspec,
                  pl.BlockSpec(bias_c.shape, lambda b: (0, 0)), pl.BlockSpec(bias_n.shape, lambda b: (0, 0))],
        out_specs=pl.BlockSpec(blk, rmap),
        out_shape=jax.ShapeDtypeStruct((t, ATT_DIM), BF16),
        compiler_params=_cparams(("arbitrary",), 56),
        name="attn_sample",
    )(q, k, v, k_cache, v_cache, bias_c, bias_n)


def _merge(ys_p, ys_s, w_ssm, w_sc, w_att, layer, gates_p, gates_s, *, tm, tn):
    (tp, k), ts = ys_p[0].shape, ys_s[0].shape[0]
    n = w_ssm.shape[2]
    ncol, npt = n // tn, tp // tm

    def pre(shared, scratch):
        @pl.when(pl.program_id(1) == 0)
        def _():
            for b in range(N_BRANCH):
                scratch[0][b] = shared[b][0].astype(BF16)

    def body(shared, ins, outs, scratch, group):
        w16_ref = scratch[0]
        m = ins[3][...] * _dot(ins[0][...], w16_ref[0])
        m = m + ins[4][...] * _dot(ins[1][...], w16_ref[1])
        m = m + ins[5][...] * _dot(ins[2][...], w16_ref[2])
        outs[0][...] = m.astype(outs[0].dtype)

    rhs = pl.BlockSpec((1, k, tn), lambda j, i: (layer, 0, j))
    lp, ls = _row_specs(tm, ts, npt, k, col_of=lambda j: 0)
    ins_p = [(y, lp) for y in ys_p]
    ins_s = [(y, ls) for y in ys_s]
    for b in range(N_BRANCH):
        gp, gs = _row_specs(tm, ts, npt, tn, col_of=lambda j, b=b: b * ncol + j)
        ins_p.append((gates_p, gp))
        ins_s.append((gates_s, gs))
    op, os_ = _row_specs(tm, ts, npt, tn)
    (mp,), (ms,) = _two_group_call(
        body, pre, grid=(ncol, npt + 1), row_axis=1, shared=[(w_ssm, rhs), (w_sc, rhs), (w_att, rhs)],
        ins_p=ins_p, ins_s=ins_s, outs_p=[(jax.ShapeDtypeStruct((tp, n), BF16), op)],
        outs_s=[(jax.ShapeDtypeStruct((ts, n), BF16), os_)],
        scratch_shapes=[pltpu.VMEM((N_BRANCH, k, tn), BF16)], vmem_mib=56, name="merge")
    return mp, ms


def _wo_body(shared, ins, outs, scratch, group):
    h = ins[0][...] + _dot(ins[1][...], shared[0][...])
    outs[0][...] = h
    ms = jnp.mean(h * h, axis=-1, keepdims=True)
    outs[1][...] = (h * lax.rsqrt(ms + EPS) * shared[1][...]).astype(outs[1].dtype)


def _wo(xp, xs, mp, ms, w_o, g, *, tm):
    (tp, d), ts = xp.shape, xs.shape[0]
    npt = tp // tm
    sp, ss = _row_specs(tm, ts, npt, d, two_d_grid=False)
    const = lambda i: (0, 0)
    (hp, hnp), (hs, hns) = _two_group_call(
        _wo_body, None, grid=(npt + 1,), row_axis=0,
        shared=[(w_o, pl.BlockSpec((d, d), const)), (g.reshape(1, d), pl.BlockSpec((1, d), const))],
        ins_p=[(xp, sp), (mp, sp)], ins_s=[(xs, ss), (ms, ss)],
        outs_p=[(jax.ShapeDtypeStruct((tp, d), F32), sp), (jax.ShapeDtypeStruct((tp, d), BF16), sp)],
        outs_s=[(jax.ShapeDtypeStruct((ts, d), F32), ss), (jax.ShapeDtypeStruct((ts, d), BF16), ss)],
        scratch_shapes=[], vmem_mib=52, name="wo")
    return (hp, hnp), (hs, hns)


def _ffn_body(shared, ins, outs, scratch, group):
    c = pl.program_id(1)
    a = _dot(ins[1][...], shared[0][...])
    a = jnp.square(jnp.maximum(a, 0.0)).astype(BF16)
    contrib = _dot(a, shared[1][...])

    @pl.when(c == 0)
    def _():
        outs[0][...] = ins[0][...] + contrib

    @pl.when(c > 0)
    def _():
        outs[0][...] += contrib


def _ffn(hp, hnp, hs, hns, w1, w2, *, tm, tc):
    (tp, d), ts = hp.shape, hs.shape[0]
    dff = w1.shape[1]
    npt = tp // tm
    sp, ss = _row_specs(tm, ts, npt, d, two_d_grid=False)
    (yp,), (ys,) = _two_group_call(
        _ffn_body, None, grid=(npt + 1, dff // tc), row_axis=0,
        shared=[(w1, pl.BlockSpec((d, tc), lambda i, c: (0, c))), (w2, pl.BlockSpec((tc, d), lambda i, c: (c, 0)))],
        ins_p=[(hp, sp), (hnp, sp)], ins_s=[(hs, ss), (hns, ss)],
        outs_p=[(jax.ShapeDtypeStruct((tp, d), F32), sp)], outs_s=[(jax.ShapeDtypeStruct((ts, d), F32), ss)],
        scratch_shapes=[], vmem_mib=52, name="ffn")
    return yp, ys


def _head_expand(lanes_per_head):
    rows = lax.broadcasted_iota(jnp.int32, (LANES, SSM_HEADS * lanes_per_head), 0)
    cols = lax.broadcasted_iota(jnp.int32, (LANES, SSM_HEADS * lanes_per_head), 1) // lanes_per_head
    return (rows == cols).astype(BF16)


def _toeplitz_bias(rel_bias):
    tbl = rel_bias.astype(F32)
    far = jnp.broadcast_to(tbl[:, 2 * MAX_REL:], (N_HEADS, ATT_PAST - MAX_REL))
    wrap = jnp.broadcast_to(tbl[:, 2 * MAX_REL:], (N_HEADS, ATT_Q_BLOCK - 1))
    n_near = TOEPLITZ_COLS + 1 - far.shape[1] - tbl.shape[1] - wrap.shape[1]
    near = jnp.broadcast_to(tbl[:, :1], (N_HEADS, n_near))
    period = jnp.concatenate([far, tbl[:, ::-1], near, wrap], axis=1)
    assert period.shape[1] == TOEPLITZ_COLS + 1
    flat = jnp.tile(period, (1, ATT_Q_BLOCK))[:, :ATT_Q_BLOCK * TOEPLITZ_COLS]
    return flat.reshape(N_HEADS, ATT_Q_BLOCK, TOEPLITZ_COLS)


def _prompt_bias(toeplitz):
    qi = jnp.arange(ATT_Q_BLOCK)[:, None]
    kj = jnp.arange(ATT_WINDOW)[None, :]
    band = kj - (qi // CHUNK) * CHUNK
    in_band = (band >= 0) & (band < ATT_BAND)
    return jnp.where(in_band[None], toeplitz[:, :, :ATT_WINDOW], NEG_INF)


def _sample_bias(toeplitz, t, lc):
    same = jnp.arange(N_HEADS)[:, None] == jnp.arange(N_HEADS)[None, :]
    tc = toeplitz[:, :t, :lc]
    bias_c = jnp.where(same[:, None, None, :], tc[:, :, :, None], NEG_INF)
    tn_ = toeplitz[:, :t, lc:lc + t]
    bias_n = jnp.where(same[:, None, :, None], tn_[:, :, None, :], NEG_INF)
    return bias_c.reshape(N_HEADS * t, lc * N_HEADS), bias_n.reshape(N_HEADS * t, N_HEADS * t)


def _pad_rows_to8(a, axis):
    pad = [(0, 0)] * a.ndim
    pad[axis] = (SUBLANES - a.shape[axis], 0)
    return jnp.pad(a, pad)


def _layer(xp, xs, lw, layer, gp, gs):
    tm, tn = PROMPT_ROW_TILE, 1024
    wt = lw["w_in_t"]
    off = IN_OFFSETS
    seq_p, seq_s = gp["seq_len"], gs["seq_len"]
    tiles_per_seq = seq_p // tm
    tail_rows = min(tm, ATT_PAST, seq_p)

    xnp, xns = _rmsnorm(xp, xs, lw["norm_mix_g"], tm)
    proj = functools.partial(_proj_act, xnp, xns, wt, layer, tm=tm)
    (zp,), (zs,) = proj(off[0], SSM_INNER, tn=tn, act=lambda a: a * jax.nn.sigmoid(a), out_kinds=["f32"],
                        name="proj_z")
    (dtp,), (dts,) = proj(off[2], LANES, tn=LANES, act=jax.nn.softplus, out_kinds=["f32"], bias=lw["dt_bias128"],
                          valid_rows=SSM_HEADS, name="proj_dt")
    (gtp,), (gts,) = proj(off[9], N_BRANCH * D_MODEL, tn=tn, act=jax.nn.sigmoid, out_kinds=["f32"],
                          name="proj_gates")
    (v16p, vtp), (v16s, vts) = proj(off[8], ATT_DIM, tn=tn, act=lambda a: a, out_kinds=["bf16", "tail"],
                                    tail_rows=tail_rows, tiles_per_seq=tiles_per_seq, name="proj_v")
    (q16p,), (q16s,) = _proj_headnorm(xnp, xns, wt, layer, off[6], ATT_DIM, lw["q_norm_g"], tm=tm, tn=tn,
                                      scale=HEAD_DIM ** -0.5, out_kinds=["bf16"], name="proj_q")
    (k16p, ktp), (k16s, kts) = _proj_headnorm(xnp, xns, wt, layer, off[7], ATT_DIM, lw["k_norm_g"], tm=tm, tn=tn,
                                              scale=1.0, out_kinds=["bf16", "tail"], tail_rows=tail_rows,
                                              tiles_per_seq=tiles_per_seq, name="proj_k")
    (xbcp, convp), (xbcs, convs) = _proj_conv(
        "xbc", xnp, xns, wt, layer, [off[1]], SSM_CONV_DIM, lw["ssm_conv_w8"], lw["ssm_conv_b"],
        gp["ssm_conv_prefix"], gs["ssm_conv_prefix"], tm=tm, tn=512, seq_p=seq_p, seq_s=seq_s, out_dtype=F32,
        name="proj_xbc")
    (yscp, scp), (yscs, scs) = _proj_conv(
        "sc", xnp, xns, wt, layer, [off[3], off[4], off[5]], SC_DIM, lw["sc_conv_w8"], None,
        gp["sc_prefix"], gs["sc_prefix"], tm=tm, tn=512, seq_p=seq_p, seq_s=seq_s, out_dtype=BF16, name="proj_sc")

    ssd = functools.partial(_ssd, a_log128=lw["a_log128"], d_x=lw["d_x"], norm_g=lw["ssm_norm_g"], ech=lw["ech"])
    yssmp, hlp = ssd(xbcp, dtp, zp, els=lw["ech"], h0=gp["ssm_h0"], L=CHUNK,
                     cps=min(SSD_CHUNKS_PER_STEP, seq_p // CHUNK), seq_len=seq_p)
    yssms, hls = ssd(xbcs, dts, zs, els=_head_expand(seq_s), h0=gs["ssm_h0"], L=seq_s, cps=1, seq_len=seq_s)

    op = _attn_prompt(q16p, k16p, v16p, lw["prompt_bias"], nseq=gp["nseq"], seq_len=seq_p)
    os_ = _attn_sample(q16s, k16s, v16s, gs["kv_cache"][0], gs["kv_cache"][1], layer, *lw["sample_bias"],
                       nseq=gs["nseq"], seq_len=seq_s)

    mtm = 512
    mp, ms = _merge((yssmp, yscp, op), (yssms, yscs, os_), lw["ssm_out_w"], lw["sc_out_w"], lw["attn_out_w"],
                    layer, gtp, gts, tm=mtm, tn=512)
    (hp, hnp), (hs, hns) = _wo(xp, xs, mp, ms, lw["w_o"], lw["norm_ffn_g"], tm=mtm)
    yp, ys = _ffn(hp, hnp, hs, hns, lw["ffn_w1"], lw["ffn_w2"], tm=mtm, tc=1024)
    return (yp, (ktp, vtp, hlp, convp, scp)), (ys, (kts, vts, hls, convs, scs))


def kernel(x_prompt, x_sample, cache_attn_k, cache_attn_v, state_ssm, state_ssm_conv, state_short_conv,
           norm_mix_g, w_in, ssm_conv_w, ssm_conv_b, ssm_dt_bias, ssm_a_log, ssm_d, ssm_norm_g, ssm_out_w,
           sc_conv_w, sc_out_w, q_norm_g, k_norm_g, rel_bias, attn_out_w, w_o, norm_ffn_g, ffn_w1, ffn_w2):
    bp, lp, d = x_prompt.shape
    bs, ls, _ = x_sample.shape
    lc = cache_attn_k.shape[2]
    assert lc == ATT_PAST and lc + ls <= TOEPLITZ_COLS and ls <= ATT_Q_BLOCK
    ech = _head_expand(SSM_HEAD_DIM)
    w_in_t = jnp.swapaxes(w_in, 1, 2)
    k_cache = cache_attn_k.reshape(DEPTH, bs, lc * N_HEADS, HEAD_DIM)
    v_cache = cache_attn_v.reshape(DEPTH, bs, lc * N_HEADS, HEAD_DIM)

    yp = x_prompt.reshape(bp * lp, d)
    ys = x_sample.reshape(bs * ls, d)
    new_p, new_s = [], []
    for l in range(DEPTH):
        toeplitz = _toeplitz_bias(rel_bias[l])
        lw = {
            "norm_mix_g": norm_mix_g[l], "norm_ffn_g": norm_ffn_g[l],
            "w_in_t": w_in_t,
            "dt_bias128": jnp.pad(ssm_dt_bias[l].astype(F32), (0, LANES - SSM_HEADS)),
            "a_log128": jnp.pad(ssm_a_log[l].astype(F32), (0, LANES - SSM_HEADS)).reshape(1, LANES),
            "d_x": jnp.repeat(ssm_d[l].astype(F32), SSM_HEAD_DIM).reshape(1, SSM_INNER),
            "ssm_norm_g": ssm_norm_g[l].astype(F32).reshape(1, SSM_INNER),
            "ssm_conv_w8": jnp.pad(ssm_conv_w[l].astype(F32), ((0, SUBLANES - SSM_CONV), (0, 0))),
            "ssm_conv_b": ssm_conv_b[l].astype(F32),
            "sc_conv_w8": jnp.pad(sc_conv_w[l].astype(F32), ((0, SUBLANES - SC_WIDTH), (0, 0))),
            "q_norm_g": q_norm_g[l].astype(F32), "k_norm_g": k_norm_g[l].astype(F32),
            "ssm_out_w": ssm_out_w, "sc_out_w": sc_out_w, "attn_out_w": attn_out_w,
            "w_o": _cast_bf16(w_o, l, tr=1024),
            "ffn_w1": _cast_bf16(ffn_w1, l, tr=256), "ffn_w2": _cast_bf16(ffn_w2, l, tr=1024),
            "ech": ech,
            "prompt_bias": _prompt_bias(toeplitz),
            "sample_bias": _sample_bias(toeplitz, ls, lc),
        }
        gp = dict(nseq=bp, seq_len=lp,
                  ssm_conv_prefix=jnp.zeros((bp, SUBLANES, SSM_CONV_DIM), F32),
                  ssm_h0=jnp.zeros((bp, SSM_INNER, SSM_STATE), F32),
                  sc_prefix=jnp.zeros((bp, SUBLANES, SC_DIM), F32))
        gs = dict(nseq=bs, seq_len=ls,
                  ssm_conv_prefix=_pad_rows_to8(state_ssm_conv[l].astype(F32), 1),
                  ssm_h0=state_ssm[l].astype(F32).reshape(bs, SSM_INNER, SSM_STATE),
                  sc_prefix=_pad_rows_to8(state_short_conv[l].astype(F32), 1),
                  kv_cache=(k_cache, v_cache))
        (yp, st_p), (ys, st_s) = _layer(yp, ys, lw, l, gp, gs)
        new_p.append(st_p)
        new_s.append(st_s)

    keep = min(ATT_PAST, lp)

    def stack(states, fn):
        return jnp.stack([fn(s) for s in states])

    hshape = lambda b: (b, SSM_HEADS, SSM_HEAD_DIM, SSM_STATE)
    return (
        yp.reshape(bp, lp, d),
        ys.reshape(bs, ls, d),
        stack(new_p, lambda s: s[0].reshape(bp, keep, N_HEADS, HEAD_DIM)),
        stack(new_p, lambda s: s[1].reshape(bp, keep, N_HEADS, HEAD_DIM)),
        stack(new_s, lambda s: s[0].reshape(bs, ls, N_HEADS, HEAD_DIM)),
        stack(new_s, lambda s: s[1].reshape(bs, ls, N_HEADS, HEAD_DIM)),
        stack(new_p, lambda s: s[2].reshape(hshape(bp))),
        stack(new_s, lambda s: s[2].reshape(hshape(bs))),
        stack(new_p, lambda s: s[3][:, SUBLANES - (SSM_CONV - 1):]),
        stack(new_s, lambda s: s[3][:, SUBLANES - (SSM_CONV - 1):]),
        stack(new_p, lambda s: s[4][:, SUBLANES - (SC_WIDTH - 1):]),
        stack(new_s, lambda s: s[4][:, SUBLANES - (SC_WIDTH - 1):]),
    )
```

```python
import functools
import math

import jax
import jax.numpy as jnp
from jax import lax
from jax.experimental import pallas as pl
from jax.experimental.pallas import tpu as pltpu

F32 = jnp.float32
BF16 = jnp.bfloat16

D_MODEL = 2048
DEPTH = 2
CHUNK = 64
EPS = 1e-6

SSM_INNER = D_MODEL
SSM_HEAD_DIM = 64
SSM_HEADS = SSM_INNER // SSM_HEAD_DIM
SSM_GROUPS = 4
SSM_STATE = 128
SSM_CONV = 4
SSM_BC = SSM_GROUPS * SSM_STATE
SSM_CONV_DIM = SSM_INNER + 2 * SSM_BC
SSM_GROUP_DIM = SSM_INNER // SSM_GROUPS

SC_DIM = D_MODEL
SC_WIDTH = 3

N_HEADS = 16
HEAD_DIM = D_MODEL // N_HEADS
ATT_DIM = N_HEADS * HEAD_DIM
ATT_PAST_CHUNKS = 8
ATT_PAST = ATT_PAST_CHUNKS * CHUNK
ATT_BAND = (ATT_PAST_CHUNKS + 1) * CHUNK
MAX_REL = 128

N_BRANCH = 3
D_FF = 4 * D_MODEL

IN_SPLITS = (SSM_INNER, SSM_CONV_DIM, SSM_HEADS, SC_DIM, SC_DIM, SC_DIM, ATT_DIM, ATT_DIM, ATT_DIM,
             N_BRANCH * D_MODEL)
IN_OFFSETS = tuple(int(sum(IN_SPLITS[:i])) for i in range(len(IN_SPLITS) + 1))

NEG_INF = -1e30

LANES = 128
SUBLANES = 8
MIB = 1024 * 1024

ATT_Q_CHUNKS = 4
ATT_Q_BLOCK = ATT_Q_CHUNKS * CHUNK
ATT_K_BLOCKS = (ATT_PAST_CHUNKS + ATT_Q_CHUNKS) // ATT_Q_CHUNKS
ATT_WINDOW = ATT_K_BLOCKS * ATT_Q_BLOCK
TOEPLITZ_COLS = 1024
SAMPLE_KEY_TILE = 2048
SSD_CHUNKS_PER_STEP = 4
PROMPT_ROW_TILE = 1024


def _cparams(semantics, vmem_mib):
    return pltpu.CompilerParams(dimension_semantics=semantics, vmem_limit_bytes=vmem_mib * MIB)


def _dot(a, b):
    return jnp.dot(a, b, preferred_element_type=F32)


def _dot_nt(a, b):
    return lax.dot_general(a, b, (((1,), (1,)), ((), ())), preferred_element_type=F32)


def _split_bf16(v, parts):
    out = []
    r = v
    for _ in range(parts):
        p = r.astype(BF16)
        out.append(p)
        r = r - p.astype(F32)
    return out


def _dot_exact_lhs(vs, m, parts):
    rows = vs[0].shape[0]
    terms = [p for v, n in zip(vs, parts) for p in _split_bf16(v, n)]
    prod = _dot(jnp.concatenate(terms, axis=0), m)
    outs, at = [], 0
    for n in parts:
        acc = prod[at * rows:(at + 1) * rows]
        for j in range(1, n):
            acc = acc + prod[(at + j) * rows:(at + j + 1) * rows]
        outs.append(acc)
        at += n
    return outs


def _dot_exact_rhs(m, v, parts):
    cols = v.shape[1]
    prod = _dot(m, jnp.concatenate(_split_bf16(v, parts), axis=1))
    acc = prod[:, 0:cols]
    for j in range(1, parts):
        acc = acc + prod[:, j * cols:(j + 1) * cols]
    return acc


def _prompt_tile(i):
    return jnp.maximum(i - 1, 0)


def _two_group_kernel(body, pre, *, n_shared, n_in, n_out, row_axis):
    def kernel(*refs):
        shared = refs[:n_shared]
        in_p = refs[n_shared:n_shared + n_in]
        in_s = refs[n_shared + n_in:n_shared + 2 * n_in]
        o0 = n_shared + 2 * n_in
        out_p = refs[o0:o0 + n_out]
        out_s = refs[o0 + n_out:o0 + 2 * n_out]
        scratch = refs[o0 + 2 * n_out:]
        if pre is not None:
            pre(shared, scratch)
        i = pl.program_id(row_axis)

        @pl.when(i == 0)
        def _():
            body(shared, in_s, out_s, scratch, 1)

        @pl.when(i > 0)
        def _():
            body(shared, in_p, out_p, scratch, 0)

    return kernel


def _two_group_call(body, pre, *, grid, row_axis, shared, ins_p, ins_s, outs_p, outs_s, scratch_shapes, vmem_mib,
                    name):
    assert len(ins_p) == len(ins_s) and len(outs_p) == len(outs_s)
    arrays = [a for a, _ in shared + ins_p + ins_s]
    in_specs = [s for _, s in shared + ins_p + ins_s]
    res = pl.pallas_call(
        _two_group_kernel(body, pre, n_shared=len(shared), n_in=len(ins_p), n_out=len(outs_p), row_axis=row_axis),
        grid=grid,
        in_specs=in_specs,
        out_specs=[s for _, s in outs_p + outs_s],
        out_shape=[a for a, _ in outs_p + outs_s],
        scratch_shapes=scratch_shapes,
        compiler_params=_cparams(("arbitrary",) * len(grid), vmem_mib),
        name=name,
    )(*arrays)
    return res[:len(outs_p)], res[len(outs_p):]


def _row_specs(tm, ts, cols, *, col_of=None, two_d_grid=True):
    if not two_d_grid:
        return (pl.BlockSpec((tm, cols), lambda i, *_: (_prompt_tile(i), 0)),
                pl.BlockSpec((ts, cols), lambda i, *_: (0, 0)))
    col_of = col_of or (lambda j: j)
    return (pl.BlockSpec((tm, cols), lambda j, i: (_prompt_tile(i), col_of(j))),
            pl.BlockSpec((ts, cols), lambda j, i: (0, col_of(j))))


def _rmsnorm_body(shared, ins, outs, scratch, group):
    x = ins[0][...]
    ms = jnp.mean(x * x, axis=-1, keepdims=True)
    outs[0][...] = (x * lax.rsqrt(ms + EPS) * shared[0][...]).astype(outs[0].dtype)


def _rmsnorm(xp, xs, g, tm):
    d = xp.shape[1]
    npt, ts = xp.shape[0] // tm, xs.shape[0]
    sp, ss = _row_specs(tm, ts, d, two_d_grid=False)
    (op,), (os_,) = _two_group_call(
        _rmsnorm_body, None, grid=(npt + 1,), row_axis=0,
        shared=[(g.reshape(1, d), pl.BlockSpec((1, d), lambda i: (0, 0)))],
        ins_p=[(xp, sp)], ins_s=[(xs, ss)],
        outs_p=[(jax.ShapeDtypeStruct(xp.shape, BF16), sp)], outs_s=[(jax.ShapeDtypeStruct(xs.shape, BF16), ss)],
        scratch_shapes=[], vmem_mib=40, name="rmsnorm")
    return op, os_


def _cast_kernel(w_ref, o_ref):
    o_ref[...] = w_ref[0].astype(o_ref.dtype)


def _cast_bf16(w, layer, *, tr):
    _, r, c = w.shape
    return pl.pallas_call(
        _cast_kernel,
        grid=(r // tr,),
        in_specs=[pl.BlockSpec((1, tr, c), lambda i: (layer, i, 0))],
        out_specs=pl.BlockSpec((tr, c), lambda i: (i, 0)),
        out_shape=jax.ShapeDtypeStruct((r, c), BF16),
        compiler_params=_cparams(("arbitrary",), 40),
        name="cast_bf16",
    )(w)


def _wspec(k, tn, layer, row0):
    assert row0 % SUBLANES == 0 and tn % SUBLANES == 0
    return pl.BlockSpec((pl.Element(1), pl.Element(tn), pl.Element(k)),
                        lambda j, i: (layer, pl.multiple_of(row0 + j * tn, SUBLANES), 0))


def _load_weight(w_ref, w16_ref, slot=None, valid_rows=None):
    @pl.when(pl.program_id(1) == 0)
    def _():
        w = w_ref[0]
        if valid_rows is not None:
            rows = lax.broadcasted_iota(jnp.int32, w.shape, 0)
            w = jnp.where(rows < valid_rows, w, 0.0)
        if slot is None:
            w16_ref[...] = w.astype(BF16)
        else:
            w16_ref[slot] = w.astype(BF16)


def _proj_outs(out_kinds, tp, ts, n, tm, tn, npt, tail_rows, tiles_per_seq):
    outs_p, outs_s = [], []
    for kind in out_kinds:
        sp, ss = _row_specs(tm, ts, tn)
        if kind == "tail":
            nseq = tp // (tm * tiles_per_seq)
            outs_p.append((jax.ShapeDtypeStruct((nseq * tail_rows, n), F32),
                           pl.BlockSpec((tail_rows, tn), lambda j, i: (_prompt_tile(i) // tiles_per_seq, j))))
            outs_s.append((jax.ShapeDtypeStruct((ts, n), F32), ss))
        else:
            dt = F32 if kind == "f32" else BF16
            outs_p.append((jax.ShapeDtypeStruct((tp, n), dt), sp))
            outs_s.append((jax.ShapeDtypeStruct((ts, n), dt), ss))
    return outs_p, outs_s


def _store_proj(y, out_kinds, outs, cols=slice(None), scale=1.0):
    for kind, o_ref in zip(out_kinds, outs):
        if kind == "tail":
            o_ref[:, cols] = y[y.shape[0] - o_ref.shape[0]:]
        elif kind == "bf16":
            o_ref[:, cols] = (y * scale).astype(BF16) if scale != 1.0 else y.astype(BF16)
        else:
            o_ref[:, cols] = y


def _proj_act(xnp, xns, wt, layer, row0, n, *, tm, tn, act, out_kinds, bias=None, valid_rows=None, tail_rows=None,
              tiles_per_seq=1, name):
    (tp, k), ts = xnp.shape, xns.shape[0]
    npt = tp // tm
    has_bias = bias is not None

    def pre(shared, scratch):
        _load_weight(shared[0], scratch[0], valid_rows=valid_rows)

    def body(shared, ins, outs, scratch, group):
        acc = _dot_nt(ins[0][...], scratch[0][...])
        if has_bias:
            acc = acc + shared[1][...]
        _store_proj(act(acc), out_kinds, outs)

    shared = [(wt, _wspec(k, tn, layer, row0))]
    if has_bias:
        shared.append((bias.reshape(1, n), pl.BlockSpec((1, tn), lambda j, i: (0, j))))
    xp_spec, xs_spec = _row_specs(tm, ts, k, col_of=lambda j: 0)
    outs_p, outs_s = _proj_outs(out_kinds, tp, ts, n, tm, tn, npt, tail_rows, tiles_per_seq)
    return _two_group_call(body, pre, grid=(n // tn, npt + 1), row_axis=1, shared=shared,
                           ins_p=[(xnp, xp_spec)], ins_s=[(xns, xs_spec)], outs_p=outs_p, outs_s=outs_s,
                           scratch_shapes=[pltpu.VMEM((tn, k), BF16)], vmem_mib=56, name=name)


def _proj_headnorm(xnp, xns, wt, layer, row0, n, g, *, tm, tn, scale, out_kinds, tail_rows=None, tiles_per_seq=1,
                   name):
    (tp, k), ts = xnp.shape, xns.shape[0]
    npt = tp // tm

    def pre(shared, scratch):
        _load_weight(shared[0], scratch[0])

    def body(shared, ins, outs, scratch, group):
        acc = _dot_nt(ins[0][...], scratch[0][...])
        gain = shared[1][...]
        for h in range(tn // HEAD_DIM):
            sl = slice(h * HEAD_DIM, (h + 1) * HEAD_DIM)
            blk = acc[:, sl]
            ms = jnp.mean(blk * blk, axis=-1, keepdims=True)
            _store_proj(blk * lax.rsqrt(ms + EPS) * gain, out_kinds, outs, cols=sl, scale=scale)

    shared = [(wt, _wspec(k, tn, layer, row0)),
              (g.reshape(1, HEAD_DIM), pl.BlockSpec((1, HEAD_DIM), lambda j, i: (0, 0)))]
    xp_spec, xs_spec = _row_specs(tm, ts, k, col_of=lambda j: 0)
    outs_p, outs_s = _proj_outs(out_kinds, tp, ts, n, tm, tn, npt, tail_rows, tiles_per_seq)
    return _two_group_call(body, pre, grid=(n // tn, npt + 1), row_axis=1, shared=shared,
                           ins_p=[(xnp, xp_spec)], ins_s=[(xns, xs_spec)], outs_p=outs_p, outs_s=outs_s,
                           scratch_shapes=[pltpu.VMEM((tn, k), BF16)], vmem_mib=56, name=name)


CONV_SUB = 256


def _causal_conv(u, cs, cw_ref, p_ref, carry_ref, st_ref, *, width, nseg, tiles_per_seq):
    tm, tn = u.shape
    seg_len = tm // nseg
    row8 = lax.broadcasted_iota(jnp.int32, (SUBLANES, tn), 0)
    outs = []
    for s in range(nseg):
        seg = u[s * seg_len:(s + 1) * seg_len]
        prev8 = carry_ref[:, cs] if tiles_per_seq > 1 else p_ref[s, :, cs]
        acc = cw_ref[width - 1:width, cs] * seg
        for k in range(1, width):
            sh = pltpu.roll(seg, k, 0)
            first8 = jnp.where(row8 < k, pltpu.roll(prev8, k, 0), sh[0:SUBLANES])
            shk = jnp.concatenate([first8, sh[SUBLANES:]], axis=0)
            acc = acc + cw_ref[width - 1 - k:width - k, cs] * shk
        outs.append(acc)
        st_ref[s, :, cs] = seg[seg_len - SUBLANES:seg_len]
    if tiles_per_seq > 1:
        carry_ref[:, cs] = u[tm - SUBLANES:tm]
    return outs[0] if nseg == 1 else jnp.concatenate(outs, axis=0)


def _proj_conv(kind, xnp, xns, wt, layer, row0s, n, conv_w8, conv_b, prefix_p, prefix_s, *, tm, tn, seq_p, seq_s,
               out_dtype, name):
    (tp, k), ts = xnp.shape, xns.shape[0]
    npt = tp // tm
    tps_p = seq_p // tm
    nseg_s = ts // seq_s
    assert tps_p >= 1 and seq_p % tm == 0 and ts % seq_s == 0
    width = SSM_CONV if kind == "xbc" else SC_WIDTH
    nw = len(row0s)
    has_b = conv_b is not None

    def pre(shared, scratch):
        for slot in range(nw):
            _load_weight(shared[slot], scratch[1], slot=slot)

    def body(shared, ins, outs, scratch, group):
        x_ref, p_ref = ins
        o_ref, st_ref = outs
        carry_ref, w16_ref = scratch
        cw_ref = shared[nw]
        nseg, tps = (1, tps_p) if group == 0 else (nseg_s, 1)
        if tps > 1:
            @pl.when(lax.rem(_prompt_tile(pl.program_id(1)), tps) == 0)
            def _():
                carry_ref[...] = p_ref[0]
        x = x_ref[...]
        for c in range(tn // CONV_SUB):
            cs = slice(c * CONV_SUB, (c + 1) * CONV_SUB)
            if kind == "xbc":
                u = _dot_nt(x, w16_ref[0, cs, :])
            else:
                u = _dot_nt(x, w16_ref[1, cs, :]) * _dot_nt(x, w16_ref[2, cs, :])
            y = _causal_conv(u, cs, cw_ref, p_ref, carry_ref, st_ref, width=width, nseg=nseg, tiles_per_seq=tps)
            if kind == "xbc":
                y = y + shared[nw + 1][:, cs]
                o_ref[:, cs] = y * jax.nn.sigmoid(y)
            else:
                o_ref[:, cs] = (_dot_nt(x, w16_ref[0, cs, :]) * y).astype(o_ref.dtype)

    shared = [(wt, _wspec(k, tn, layer, r)) for r in row0s]
    shared.append((conv_w8, pl.BlockSpec((SUBLANES, tn), lambda j, i: (0, j))))
    if has_b:
        shared.append((conv_b.reshape(1, n), pl.BlockSpec((1, tn), lambda j, i: (0, j))))
    xp_spec, xs_spec = _row_specs(tm, ts, k, col_of=lambda j: 0)
    yp_spec, ys_spec = _row_specs(tm, ts, tn)
    ins_p = [(xnp, xp_spec),
             (prefix_p, pl.BlockSpec((1, SUBLANES, tn), lambda j, i: (_prompt_tile(i) // tps_p, 0, j)))]
    ins_s = [(xns, xs_spec), (prefix_s, pl.BlockSpec((nseg_s, SUBLANES, tn), lambda j, i: (0, 0, j)))]
    outs_p = [(jax.ShapeDtypeStruct((tp, n), out_dtype), yp_spec),
              (jax.ShapeDtypeStruct((npt, SUBLANES, n), F32),
               pl.BlockSpec((1, SUBLANES, tn), lambda j, i: (_prompt_tile(i), 0, j)))]
    outs_s = [(jax.ShapeDtypeStruct((ts, n), out_dtype), ys_spec),
              (jax.ShapeDtypeStruct((nseg_s, SUBLANES, n), F32),
               pl.BlockSpec((nseg_s, SUBLANES, tn), lambda j, i: (0, 0, j)))]
    (yp, tails_p), (ys, tails_s) = _two_group_call(
        body, pre, grid=(n // tn, npt + 1), row_axis=1, shared=shared, ins_p=ins_p, ins_s=ins_s, outs_p=outs_p,
        outs_s=outs_s, scratch_shapes=[pltpu.VMEM((SUBLANES, tn), F32), pltpu.VMEM((nw, tn, k), BF16)],
        vmem_mib=56, name=name)
    return (yp, tails_p[tps_p - 1::tps_p]), (ys, tails_s)


def _ssd_kernel(x_ref, b_ref, c_ref, dt_ref, zs_ref, alog_ref, dx_ref, ng_ref, ech_ref, els_ref, h0_ref,
                y_ref, hl_ref, ht_ref, *, L, cps, nsteps):
    ci = pl.program_id(1)
    hp = LANES // L
    ntiles = SSM_HEADS // hp
    gw = SSM_GROUP_DIM
    tw = hp * SSM_HEAD_DIM
    log2_l = L.bit_length() - 1

    @pl.when(ci == 0)
    def _():
        ht_ref[...] = h0_ref[0].T

    a = -jnp.exp(alog_ref[...])
    ri = lax.broadcasted_iota(jnp.int32, (L, L), 0)
    cj = lax.broadcasted_iota(jnp.int32, (L, L), 1)
    tri = (ri >= cj).astype(BF16)
    lane_blk = jnp.right_shift(lax.broadcasted_iota(jnp.int32, (1, LANES), 1), log2_l)
    row_l = lax.broadcasted_iota(jnp.int32, (L, LANES), 0)
    lane_s = jnp.bitwise_and(lax.broadcasted_iota(jnp.int32, (L, LANES), 1), L - 1)
    causal = row_l >= lane_s
    rb = jnp.right_shift(lax.broadcasted_iota(jnp.int32, (LANES, tw), 0), log2_l)
    cb_ = jnp.right_shift(lax.broadcasted_iota(jnp.int32, (LANES, tw), 1), SSM_HEAD_DIM.bit_length() - 1)
    blockdiag = rb == cb_
    tile_rows = lambda v: jnp.concatenate([v] * hp, axis=0)
    zpad = jnp.zeros((LANES - L, gw), F32)

    for r in range(cps):
        rows = slice(r * L, (r + 1) * L)
        x = x_ref[rows, :]
        bm = b_ref[rows, :]
        cm = c_ref[rows, :]
        dt = dt_ref[rows, :]
        da = dt * a

        acum = _dot_exact_rhs(tri, da, 3)
        eacum = jnp.exp(acum)
        dend = jnp.exp(acum[L - 1:L, :] - acum)
        w = dt * dend

        ech = ech_ref[...]
        if L == SSM_HEAD_DIM:
            cexp = jnp.concatenate(
                [jnp.broadcast_to(acum[:, h:h + 1], (L, SSM_HEAD_DIM)) for h in range(SSM_HEADS)], axis=1)
        else:
            (cexp,) = _dot_exact_lhs([acum], els_ref[...], [3])
        wx, ex = _dot_exact_lhs([w, eacum], ech, [2, 2])

        acum_t = tile_rows(acum).T
        dt_t = tile_rows(dt).T

        bsq = [tile_rows(bm[:, g * SSM_STATE:(g + 1) * SSM_STATE]) for g in range(SSM_GROUPS)]
        cbt = [_dot_nt(cm[:, g * SSM_STATE:(g + 1) * SSM_STATE].astype(BF16), bsq[g].astype(BF16))
               for g in range(SSM_GROUPS)]

        yd = []
        for t in range(ntiles):
            h_first = t * hp
            g = h_first // (SSM_HEADS // SSM_GROUPS)
            r_row = acum_t[h_first:h_first + 1, :]
            d_row = dt_t[h_first:h_first + 1, :]
            for jj in range(1, hp):
                sel = lane_blk == jj
                r_row = jnp.where(sel, acum_t[h_first + jj:h_first + jj + 1, :], r_row)
                d_row = jnp.where(sel, dt_t[h_first + jj:h_first + jj + 1, :], d_row)
            diff = cexp[:, t * LANES:(t + 1) * LANES] - r_row
            dec = jnp.exp(jnp.where(causal, diff, -jnp.inf))
            sc = (cbt[g] * dec * d_row).astype(BF16)
            xs = tile_rows(x[:, t * tw:(t + 1) * tw])
            rhs = jnp.where(blockdiag, xs, 0.0).astype(BF16)
            yd.append(_dot(sc, rhs))
        y = jnp.concatenate(yd, axis=1)

        xw = x * wx
        for g in range(SSM_GROUPS):
            gs = slice(g * gw, (g + 1) * gw)
            h_in = ht_ref[:, gs]
            y_off = _dot(cm[:, g * SSM_STATE:(g + 1) * SSM_STATE].astype(BF16), h_in.astype(BF16))
            yg = y[:, gs] + y_off * ex[:, gs] + dx_ref[:, gs] * x[:, gs]
            yg = yg * zs_ref[rows, gs]
            ms = jnp.mean(yg * yg, axis=-1, keepdims=True)
            y_ref[rows, gs] = (yg * lax.rsqrt(ms + EPS) * ng_ref[:, gs]).astype(y_ref.dtype)
            bm_t = bsq[g].T.astype(BF16)
            upd = jnp.concatenate([xw[:, gs], zpad], axis=0).astype(BF16)
            ht_ref[:, gs] = h_in * ex[L - 1:L, gs] + _dot(bm_t, upd)

    @pl.when(ci == nsteps - 1)
    def _():
        hl_ref[0] = ht_ref[...].T


def _ssd(xbc_act, dt, zs, a_log128, d_x, norm_g, ech, els, h0, *, L, cps, seq_len):
    t = xbc_act.shape[0]
    nseq = h0.shape[0]
    rows = L * cps
    ns = seq_len // rows
    rmap = lambda b, c: (b * ns + c, 0)
    cmap = lambda b, c: (0, 0)
    nb = SSM_INNER // SSM_BC
    return pl.pallas_call(
        functools.partial(_ssd_kernel, L=L, cps=cps, nsteps=ns),
        grid=(nseq, ns),
        in_specs=[pl.BlockSpec((rows, SSM_INNER), rmap),
                  pl.BlockSpec((rows, SSM_BC), lambda b, c: (b * ns + c, nb)),
                  pl.BlockSpec((rows, SSM_BC), lambda b, c: (b * ns + c, nb + 1)),
                  pl.BlockSpec((rows, LANES), rmap),
                  pl.BlockSpec((rows, SSM_INNER), rmap),
                  pl.BlockSpec((1, LANES), cmap),
                  pl.BlockSpec((1, SSM_INNER), cmap),
                  pl.BlockSpec((1, SSM_INNER), cmap),
                  pl.BlockSpec(ech.shape, cmap),
                  pl.BlockSpec(els.shape, cmap),
                  pl.BlockSpec((1, SSM_INNER, SSM_STATE), lambda b, c: (b, 0, 0))],
        out_specs=[pl.BlockSpec((rows, SSM_INNER), rmap),
                   pl.BlockSpec((1, SSM_INNER, SSM_STATE), lambda b, c: (b, 0, 0))],
        out_shape=[jax.ShapeDtypeStruct((t, SSM_INNER), BF16),
                   jax.ShapeDtypeStruct((nseq, SSM_INNER, SSM_STATE), F32)],
        scratch_shapes=[pltpu.VMEM((SSM_STATE, SSM_INNER), F32)],
        compiler_params=_cparams(("arbitrary", "arbitrary"), 48),
        name=f"ssd_L{L}",
    )(xbc_act, xbc_act, xbc_act, dt, zs, a_log128, d_x, norm_g, ech, els, h0)


def _softmax_pv(s_parts, v_parts):
    m = functools.reduce(jnp.maximum, [jnp.max(s, axis=1, keepdims=True) for s in s_parts])
    l = None
    o = None
    for s, v in zip(s_parts, v_parts):
        p = jnp.exp(s - m)
        ls = jnp.sum(p, axis=1, keepdims=True)
        os_ = _dot(p.astype(BF16), v)
        l = ls if l is None else l + ls
        o = os_ if o is None else o + os_
    return o / l


def _attn_prompt_kernel(q_ref, k0_ref, k1_ref, k2_ref, v0_ref, v1_ref, v2_ref, bias_ref, o_ref):
    i = pl.program_id(1)
    k_refs = (k0_ref, k1_ref, k2_ref)
    v_refs = (v0_ref, v1_ref, v2_ref)
    first_valid = (ATT_K_BLOCKS - 1 - i) * ATT_Q_BLOCK
    kidx = lax.broadcasted_iota(jnp.int32, (ATT_Q_BLOCK, ATT_Q_BLOCK), 1)
    for h in range(N_HEADS):
        sl = slice(h * HEAD_DIM, (h + 1) * HEAD_DIM)
        q = q_ref[:, sl]
        s_parts = []
        for kb in range(ATT_K_BLOCKS):
            s = _dot_nt(q, k_refs[kb][:, sl]) + bias_ref[h, :, kb * ATT_Q_BLOCK:(kb + 1) * ATT_Q_BLOCK]
            s_parts.append(jnp.where(kidx + kb * ATT_Q_BLOCK >= first_valid, s, NEG_INF))
        o = _softmax_pv(s_parts, [v_refs[kb][:, sl] for kb in range(ATT_K_BLOCKS)])
        o_ref[:, sl] = o.astype(o_ref.dtype)


def _attn_prompt(q, k, v, bias, *, nseq, seq_len):
    t = q.shape[0]
    nqb = seq_len // ATT_Q_BLOCK
    qmap = lambda b, i: (b * nqb + i, 0)

    def kmap(back):
        return lambda b, i: (b * nqb + jnp.maximum(i - back, 0), 0)

    blk = (ATT_Q_BLOCK, ATT_DIM)
    kv_specs = [pl.BlockSpec(blk, kmap(ATT_K_BLOCKS - 1 - kb)) for kb in range(ATT_K_BLOCKS)]
    return pl.pallas_call(
        _attn_prompt_kernel,
        grid=(nseq, nqb),
        in_specs=[pl.BlockSpec(blk, qmap)] + kv_specs + kv_specs
        + [pl.BlockSpec(bias.shape, lambda b, i: (0, 0, 0))],
        out_specs=pl.BlockSpec(blk, qmap),
        out_shape=jax.ShapeDtypeStruct((t, ATT_DIM), BF16),
        compiler_params=_cparams(("arbitrary", "arbitrary"), 56),
        name="attn_prompt",
    )(q, k, k, k, v, v, v, bias)


def _heads_to_rows(ref):
    return jnp.concatenate([ref[:, h * HEAD_DIM:(h + 1) * HEAD_DIM] for h in range(N_HEADS)], axis=0)


def _attn_sample_kernel(q_ref, kn_ref, vn_ref, kc_ref, vc_ref, bf_ref, bc_ref, bn_ref, o_ref):
    t = q_ref.shape[0]
    q2 = _heads_to_rows(q_ref)
    kn2, vn2 = _heads_to_rows(kn_ref), _heads_to_rows(vn_ref)
    npos = SAMPLE_KEY_TILE // N_HEADS
    ntiles = kc_ref.shape[2] // npos
    nfar = ntiles - bc_ref.shape[1] // SAMPLE_KEY_TILE
    assert t & (t - 1) == 0
    row_head = jnp.right_shift(lax.broadcasted_iota(jnp.int32, (N_HEADS * t, SAMPLE_KEY_TILE), 0),
                               t.bit_length() - 1)
    col_head = jnp.bitwise_and(lax.broadcasted_iota(jnp.int32, (N_HEADS * t, SAMPLE_KEY_TILE), 1), N_HEADS - 1)
    far_bias = jnp.where(row_head == col_head, bf_ref[...], NEG_INF)
    s_parts, v_parts = [], []
    for c in range(ntiles):
        ps = slice(c * npos, (c + 1) * npos)
        kc = kc_ref[0, 0, ps, :, :].reshape(SAMPLE_KEY_TILE, HEAD_DIM)
        vc = vc_ref[0, 0, ps, :, :].reshape(SAMPLE_KEY_TILE, HEAD_DIM)
        if c < nfar:
            bias = far_bias
        else:
            bias = bc_ref[:, (c - nfar) * SAMPLE_KEY_TILE:(c - nfar + 1) * SAMPLE_KEY_TILE]
        s_parts.append(_dot_nt(q2, kc.astype(BF16)) + bias)
        v_parts.append(vc.astype(BF16))
    s_parts.append(_dot_nt(q2, kn2) + bn_ref[...])
    v_parts.append(vn2)
    o2 = _softmax_pv(s_parts, v_parts)
    for h in range(N_HEADS):
        o_ref[:, h * HEAD_DIM:(h + 1) * HEAD_DIM] = o2[h * t:(h + 1) * t].astype(o_ref.dtype)


def _attn_sample(q, k, v, k_cache, v_cache, layer, bias_far, bias_c, bias_n, *, nseq, seq_len):
    t = q.shape[0]
    blk = (seq_len, ATT_DIM)
    rmap = lambda b: (b, 0)
    const = lambda b: (0, 0)
    cspec = pl.BlockSpec((1, 1) + k_cache.shape[2:], lambda b: (layer, b, 0, 0, 0))
    return pl.pallas_call(
        _attn_sample_kernel,
        grid=(nseq,),
        in_specs=[pl.BlockSpec(blk, rmap), pl.BlockSpec(blk, rmap), pl.BlockSpec(blk, rmap), cspec, cspec,
                  pl.BlockSpec(bias_far.shape, const), pl.BlockSpec(bias_c.shape, const),
                  pl.BlockSpec(bias_n.shape, const)],
        out_specs=pl.BlockSpec(blk, rmap),
        out_shape=jax.ShapeDtypeStruct((t, ATT_DIM), BF16),
        compiler_params=_cparams(("arbitrary",), 56),
        name="attn_sample",
    )(q, k, v, k_cache, v_cache, bias_far, bias_c, bias_n)


def _merge(ys_p, ys_s, w_ssm, w_sc, w_att, layer, gates_p, gates_s, *, tm, tn):
    (tp, k), ts = ys_p[0].shape, ys_s[0].shape[0]
    n = w_ssm.shape[2]
    ncol, npt = n // tn, tp // tm

    def pre(shared, scratch):
        @pl.when(pl.program_id(1) == 0)
        def _():
            for b in range(N_BRANCH):
                scratch[0][b] = shared[b][0].astype(BF16)

    def body(shared, ins, outs, scratch, group):
        w16_ref = scratch[0]
        m = ins[3][...] * _dot(ins[0][...], w16_ref[0])
        m = m + ins[4][...] * _dot(ins[1][...], w16_ref[1])
        m = m + ins[5][...] * _dot(ins[2][...], w16_ref[2])
        outs[0][...] = m.astype(outs[0].dtype)

    rhs = pl.BlockSpec((1, k, tn), lambda j, i: (layer, 0, j))
    lp, ls = _row_specs(tm, ts, k, col_of=lambda j: 0)
    ins_p = [(y, lp) for y in ys_p]
    ins_s = [(y, ls) for y in ys_s]
    for b in range(N_BRANCH):
        gp, gs = _row_specs(tm, ts, tn, col_of=lambda j, b=b: b * ncol + j)
        ins_p.append((gates_p, gp))
        ins_s.append((gates_s, gs))
    op, os_ = _row_specs(tm, ts, tn)
    (mp,), (ms,) = _two_group_call(
        body, pre, grid=(ncol, npt + 1), row_axis=1, shared=[(w_ssm, rhs), (w_sc, rhs), (w_att, rhs)],
        ins_p=ins_p, ins_s=ins_s, outs_p=[(jax.ShapeDtypeStruct((tp, n), BF16), op)],
        outs_s=[(jax.ShapeDtypeStruct((ts, n), BF16), os_)],
        scratch_shapes=[pltpu.VMEM((N_BRANCH, k, tn), BF16)], vmem_mib=56, name="merge")
    return mp, ms


def _wo_body(shared, ins, outs, scratch, group):
    h = ins[0][...] + _dot(ins[1][...], shared[0][...])
    outs[0][...] = h
    ms = jnp.mean(h * h, axis=-1, keepdims=True)
    outs[1][...] = (h * lax.rsqrt(ms + EPS) * shared[1][...]).astype(outs[1].dtype)


def _wo(xp, xs, mp, ms, w_o, g, *, tm):
    (tp, d), ts = xp.shape, xs.shape[0]
    npt = tp // tm
    sp, ss = _row_specs(tm, ts, d, two_d_grid=False)
    const = lambda i: (0, 0)
    (hp, hnp), (hs, hns) = _two_group_call(
        _wo_body, None, grid=(npt + 1,), row_axis=0,
        shared=[(w_o, pl.BlockSpec((d, d), const)), (g.reshape(1, d), pl.BlockSpec((1, d), const))],
        ins_p=[(xp, sp), (mp, sp)], ins_s=[(xs, ss), (ms, ss)],
        outs_p=[(jax.ShapeDtypeStruct((tp, d), F32), sp), (jax.ShapeDtypeStruct((tp, d), BF16), sp)],
        outs_s=[(jax.ShapeDtypeStruct((ts, d), F32), ss), (jax.ShapeDtypeStruct((ts, d), BF16), ss)],
        scratch_shapes=[], vmem_mib=52, name="wo")
    return (hp, hnp), (hs, hns)


def _ffn_body(shared, ins, outs, scratch, group):
    c = pl.program_id(1)
    a = _dot(ins[1][...], shared[0][...])
    a = jnp.square(jnp.maximum(a, 0.0)).astype(BF16)
    contrib = _dot(a, shared[1][...])

    @pl.when(c == 0)
    def _():
        outs[0][...] = ins[0][...] + contrib

    @pl.when(c > 0)
    def _():
        outs[0][...] += contrib


def _ffn(hp, hnp, hs, hns, w1, w2, *, tm, tc):
    (tp, d), ts = hp.shape, hs.shape[0]
    dff = w1.shape[1]
    npt = tp // tm
    sp, ss = _row_specs(tm, ts, d, two_d_grid=False)
    (yp,), (ys,) = _two_group_call(
        _ffn_body, None, grid=(npt + 1, dff // tc), row_axis=0,
        shared=[(w1, pl.BlockSpec((d, tc), lambda i, c: (0, c))), (w2, pl.BlockSpec((tc, d), lambda i, c: (c, 0)))],
        ins_p=[(hp, sp), (hnp, sp)], ins_s=[(hs, ss), (hns, ss)],
        outs_p=[(jax.ShapeDtypeStruct((tp, d), F32), sp)], outs_s=[(jax.ShapeDtypeStruct((ts, d), F32), ss)],
        scratch_shapes=[], vmem_mib=52, name="ffn")
    return yp, ys


def _head_expand(lanes_per_head):
    rows = lax.broadcasted_iota(jnp.int32, (LANES, SSM_HEADS * lanes_per_head), 0)
    cols = lax.broadcasted_iota(jnp.int32, (LANES, SSM_HEADS * lanes_per_head), 1) // lanes_per_head
    return (rows == cols).astype(BF16)


def _toeplitz_bias(rel_bias):
    tbl = rel_bias.astype(F32)
    far = jnp.broadcast_to(tbl[:, 2 * MAX_REL:], (N_HEADS, ATT_PAST - MAX_REL))
    wrap = jnp.broadcast_to(tbl[:, 2 * MAX_REL:], (N_HEADS, ATT_Q_BLOCK - 1))
    n_near = TOEPLITZ_COLS + 1 - far.shape[1] - tbl.shape[1] - wrap.shape[1]
    near = jnp.broadcast_to(tbl[:, :1], (N_HEADS, n_near))
    period = jnp.concatenate([far, tbl[:, ::-1], near, wrap], axis=1)
    assert period.shape[1] == TOEPLITZ_COLS + 1
    flat = jnp.tile(period, (1, ATT_Q_BLOCK))[:, :ATT_Q_BLOCK * TOEPLITZ_COLS]
    return flat.reshape(N_HEADS, ATT_Q_BLOCK, TOEPLITZ_COLS)


def _prompt_bias(toeplitz):
    qi = jnp.arange(ATT_Q_BLOCK)[:, None]
    kj = jnp.arange(ATT_WINDOW)[None, :]
    band = kj - (qi // CHUNK) * CHUNK
    in_band = (band >= 0) & (band < ATT_BAND)
    return jnp.where(in_band[None], toeplitz[:, :, :ATT_WINDOW], NEG_INF)


def _sample_bias(toeplitz, t, lc):
    npos = SAMPLE_KEY_TILE // N_HEADS
    far = (lc - MAX_REL) // npos * npos
    same = jnp.arange(N_HEADS)[:, None] == jnp.arange(N_HEADS)[None, :]
    bias_far = jnp.broadcast_to(toeplitz[:, :1, :1], (N_HEADS, t, 1))
    tc = toeplitz[:, :t, far:lc]
    bias_c = jnp.where(same[:, None, None, :], tc[:, :, :, None], NEG_INF)
    tn_ = toeplitz[:, :t, lc:lc + t]
    bias_n = jnp.where(same[:, None, :, None], tn_[:, :, None, :], NEG_INF)
    return (bias_far.reshape(N_HEADS * t, 1), bias_c.reshape(N_HEADS * t, (lc - far) * N_HEADS),
            bias_n.reshape(N_HEADS * t, N_HEADS * t))


def _pad_rows_to8(a, axis):
    pad = [(0, 0)] * a.ndim
    pad[axis] = (SUBLANES - a.shape[axis], 0)
    return jnp.pad(a, pad)


def _layer(xp, xs, lw, layer, gp, gs):
    tm, tn = PROMPT_ROW_TILE, 1024
    wt = lw["w_in_t"]
    off = IN_OFFSETS
    seq_p, seq_s = gp["seq_len"], gs["seq_len"]
    tiles_per_seq = seq_p // tm
    tail_rows = min(tm, ATT_PAST, seq_p)

    xnp, xns = _rmsnorm(xp, xs, lw["norm_mix_g"], tm)
    proj = functools.partial(_proj_act, xnp, xns, wt, layer, tm=tm)
    (zp,), (zs,) = proj(off[0], SSM_INNER, tn=tn, act=lambda a: a * jax.nn.sigmoid(a), out_kinds=["f32"],
                        name="proj_z")
    (dtp,), (dts,) = proj(off[2], LANES, tn=LANES, act=jax.nn.softplus, out_kinds=["f32"], bias=lw["dt_bias128"],
                          valid_rows=SSM_HEADS, name="proj_dt")
    (gtp,), (gts,) = proj(off[9], N_BRANCH * D_MODEL, tn=tn, act=jax.nn.sigmoid, out_kinds=["f32"],
                          name="proj_gates")
    (v16p, vtp), (v16s, vts) = proj(off[8], ATT_DIM, tn=tn, act=lambda a: a, out_kinds=["bf16", "tail"],
                                    tail_rows=tail_rows, tiles_per_seq=tiles_per_seq, name="proj_v")
    (q16p,), (q16s,) = _proj_headnorm(xnp, xns, wt, layer, off[6], ATT_DIM, lw["q_norm_g"], tm=tm, tn=tn,
                                      scale=HEAD_DIM ** -0.5, out_kinds=["bf16"], name="proj_q")
    (k16p, ktp), (k16s, kts) = _proj_headnorm(xnp, xns, wt, layer, off[7], ATT_DIM, lw["k_norm_g"], tm=tm, tn=tn,
                                              scale=1.0, out_kinds=["bf16", "tail"], tail_rows=tail_rows,
                                              tiles_per_seq=tiles_per_seq, name="proj_k")
    (xbcp, convp), (xbcs, convs) = _proj_conv(
        "xbc", xnp, xns, wt, layer, [off[1]], SSM_CONV_DIM, lw["ssm_conv_w8"], lw["ssm_conv_b"],
        gp["ssm_conv_prefix"], gs["ssm_conv_prefix"], tm=tm, tn=512, seq_p=seq_p, seq_s=seq_s, out_dtype=F32,
        name="proj_xbc")
    (yscp, scp), (yscs, scs) = _proj_conv(
        "sc", xnp, xns, wt, layer, [off[3], off[4], off[5]], SC_DIM, lw["sc_conv_w8"], None,
        gp["sc_prefix"], gs["sc_prefix"], tm=tm, tn=512, seq_p=seq_p, seq_s=seq_s, out_dtype=BF16, name="proj_sc")

    ssd = functools.partial(_ssd, a_log128=lw["a_log128"], d_x=lw["d_x"], norm_g=lw["ssm_norm_g"], ech=lw["ech"])
    yssmp, hlp = ssd(xbcp, dtp, zp, els=lw["ech"], h0=gp["ssm_h0"], L=CHUNK,
                     cps=min(SSD_CHUNKS_PER_STEP, seq_p // CHUNK), seq_len=seq_p)
    yssms, hls = ssd(xbcs, dts, zs, els=_head_expand(seq_s), h0=gs["ssm_h0"], L=seq_s, cps=1, seq_len=seq_s)

    op = _attn_prompt(q16p, k16p, v16p, lw["prompt_bias"], nseq=gp["nseq"], seq_len=seq_p)
    os_ = _attn_sample(q16s, k16s, v16s, gs["kv_cache"][0], gs["kv_cache"][1], layer, *lw["sample_bias"],
                       nseq=gs["nseq"], seq_len=seq_s)

    mtm = 512
    mp, ms = _merge((yssmp, yscp, op), (yssms, yscs, os_), lw["ssm_out_w"], lw["sc_out_w"], lw["attn_out_w"],
                    layer, gtp, gts, tm=mtm, tn=512)
    (hp, hnp), (hs, hns) = _wo(xp, xs, mp, ms, lw["w_o"], lw["norm_ffn_g"], tm=mtm)
    yp, ys = _ffn(hp, hnp, hs, hns, lw["ffn_w1"], lw["ffn_w2"], tm=mtm, tc=1024)
    return (yp, (ktp, vtp, hlp, convp, scp)), (ys, (kts, vts, hls, convs, scs))


def kernel(x_prompt, x_sample, cache_attn_k, cache_attn_v, state_ssm, state_ssm_conv, state_short_conv,
           norm_mix_g, w_in, ssm_conv_w, ssm_conv_b, ssm_dt_bias, ssm_a_log, ssm_d, ssm_norm_g, ssm_out_w,
           sc_conv_w, sc_out_w, q_norm_g, k_norm_g, rel_bias, attn_out_w, w_o, norm_ffn_g, ffn_w1, ffn_w2):
    bp, lp, d = x_prompt.shape
    bs, ls, _ = x_sample.shape
    lc = cache_attn_k.shape[2]
    assert lc == ATT_PAST and lc + ls <= TOEPLITZ_COLS and ls <= ATT_Q_BLOCK
    ech = _head_expand(SSM_HEAD_DIM)
    w_in_t = jnp.swapaxes(w_in, 1, 2)

    yp = x_prompt.reshape(bp * lp, d)
    ys = x_sample.reshape(bs * ls, d)
    new_p, new_s = [], []
    for l in range(DEPTH):
        toeplitz = _toeplitz_bias(rel_bias[l])
        lw = {
            "norm_mix_g": norm_mix_g[l], "norm_ffn_g": norm_ffn_g[l],
            "w_in_t": w_in_t,
            "dt_bias128": jnp.pad(ssm_dt_bias[l].astype(F32), (0, LANES - SSM_HEADS)),
            "a_log128": jnp.pad(ssm_a_log[l].astype(F32), (0, LANES - SSM_HEADS)).reshape(1, LANES),
            "d_x": jnp.repeat(ssm_d[l].astype(F32), SSM_HEAD_DIM).reshape(1, SSM_INNER),
            "ssm_norm_g": ssm_norm_g[l].astype(F32).reshape(1, SSM_INNER),
            "ssm_conv_w8": jnp.pad(ssm_conv_w[l].astype(F32), ((0, SUBLANES - SSM_CONV), (0, 0))),
            "ssm_conv_b": ssm_conv_b[l].astype(F32),
            "sc_conv_w8": jnp.pad(sc_conv_w[l].astype(F32), ((0, SUBLANES - SC_WIDTH), (0, 0))),
            "q_norm_g": q_norm_g[l].astype(F32), "k_norm_g": k_norm_g[l].astype(F32),
            "ssm_out_w": ssm_out_w, "sc_out_w": sc_out_w, "attn_out_w": attn_out_w,
            "w_o": _cast_bf16(w_o, l, tr=1024),
            "ffn_w1": _cast_bf16(ffn_w1, l, tr=256), "ffn_w2": _cast_bf16(ffn_w2, l, tr=1024),
            "ech": ech,
            "prompt_bias": _prompt_bias(toeplitz),
            "sample_bias": _sample_bias(toeplitz, ls, lc),
        }
        gp = dict(nseq=bp, seq_len=lp,
                  ssm_conv_prefix=jnp.zeros((bp, SUBLANES, SSM_CONV_DIM), F32),
                  ssm_h0=jnp.zeros((bp, SSM_INNER, SSM_STATE), F32),
                  sc_prefix=jnp.zeros((bp, SUBLANES, SC_DIM), F32))
        gs = dict(nseq=bs, seq_len=ls,
                  ssm_conv_prefix=_pad_rows_to8(state_ssm_conv[l].astype(F32), 1),
                  ssm_h0=state_ssm[l].astype(F32).reshape(bs, SSM_INNER, SSM_STATE),
                  sc_prefix=_pad_rows_to8(state_short_conv[l].astype(F32), 1),
                  kv_cache=(cache_attn_k, cache_attn_v))
        (yp, st_p), (ys, st_s) = _layer(yp, ys, lw, l, gp, gs)
        new_p.append(st_p)
        new_s.append(st_s)

    keep = min(ATT_PAST, lp)

    def stack(states, fn):
        return jnp.stack([fn(s) for s in states])

    hshape = lambda b: (b, SSM_HEADS, SSM_HEAD_DIM, SSM_STATE)
    return (
        yp.reshape(bp, lp, d),
        ys.reshape(bs, ls, d),
        stack(new_p, lambda s: s[0].reshape(bp, keep, N_HEADS, HEAD_DIM)),
        stack(new_p, lambda s: s[1].reshape(bp, keep, N_HEADS, HEAD_DIM)),
        stack(new_s, lambda s: s[0].reshape(bs, ls, N_HEADS, HEAD_DIM)),
        stack(new_s, lambda s: s[1].reshape(bs, ls, N_HEADS, HEAD_DIM)),
        stack(new_p, lambda s: s[2].reshape(hshape(bp))),
        stack(new_s, lambda s: s[2].reshape(hshape(bs))),
        stack(new_p, lambda s: s[3][:, SUBLANES - (SSM_CONV - 1):]),
        stack(new_s, lambda s: s[3][:, SUBLANES - (SSM_CONV - 1):]),
        stack(new_p, lambda s: s[4][:, SUBLANES - (SC_WIDTH - 1):]),
        stack(new_s, lambda s: s[4][:, SUBLANES - (SC_WIDTH - 1):]),
    )
```

```python
import functools
import math

import jax
import jax.numpy as jnp
from jax import lax
from jax.experimental import pallas as pl
from jax.experimental.pallas import tpu as pltpu

F32 = jnp.float32
BF16 = jnp.bfloat16

D_MODEL = 2048
DEPTH = 2
CHUNK = 64
EPS = 1e-6

SSM_INNER = D_MODEL
SSM_HEAD_DIM = 64
SSM_HEADS = SSM_INNER // SSM_HEAD_DIM
SSM_GROUPS = 4
SSM_STATE = 128
SSM_CONV = 4
SSM_BC = SSM_GROUPS * SSM_STATE
SSM_CONV_DIM = SSM_INNER + 2 * SSM_BC
SSM_GROUP_DIM = SSM_INNER // SSM_GROUPS

SC_DIM = D_MODEL
SC_WIDTH = 3

N_HEADS = 16
HEAD_DIM = D_MODEL // N_HEADS
ATT_DIM = N_HEADS * HEAD_DIM
ATT_PAST_CHUNKS = 8
ATT_PAST = ATT_PAST_CHUNKS * CHUNK
ATT_BAND = (ATT_PAST_CHUNKS + 1) * CHUNK
MAX_REL = 128

N_BRANCH = 3
D_FF = 4 * D_MODEL

IN_SPLITS = (SSM_INNER, SSM_CONV_DIM, SSM_HEADS, SC_DIM, SC_DIM, SC_DIM, ATT_DIM, ATT_DIM, ATT_DIM,
             N_BRANCH * D_MODEL)
IN_OFFSETS = tuple(int(sum(IN_SPLITS[:i])) for i in range(len(IN_SPLITS) + 1))

NEG_INF = -1e30

LANES = 128
SUBLANES = 8
MIB = 1024 * 1024

ATT_Q_CHUNKS = 4
ATT_Q_BLOCK = ATT_Q_CHUNKS * CHUNK
ATT_K_BLOCKS = (ATT_PAST_CHUNKS + ATT_Q_CHUNKS) // ATT_Q_CHUNKS
ATT_WINDOW = ATT_K_BLOCKS * ATT_Q_BLOCK
TOEPLITZ_COLS = 1024
SAMPLE_KEY_TILE = 2048
SSD_CHUNKS_PER_STEP = 4
PROMPT_ROW_TILE = 1024


def _cparams(semantics, vmem_mib):
    return pltpu.CompilerParams(dimension_semantics=semantics, vmem_limit_bytes=vmem_mib * MIB)


def _dot(a, b):
    return jnp.dot(a, b, preferred_element_type=F32)


def _dot_nt(a, b):
    return lax.dot_general(a, b, (((1,), (1,)), ((), ())), preferred_element_type=F32)


def _split_bf16(v, parts):
    out = []
    r = v
    for _ in range(parts):
        p = r.astype(BF16)
        out.append(p)
        r = r - p.astype(F32)
    return out


def _dot_exact_lhs(vs, m, parts):
    rows = vs[0].shape[0]
    terms = [p for v, n in zip(vs, parts) for p in _split_bf16(v, n)]
    prod = _dot(jnp.concatenate(terms, axis=0), m)
    outs, at = [], 0
    for n in parts:
        acc = prod[at * rows:(at + 1) * rows]
        for j in range(1, n):
            acc = acc + prod[(at + j) * rows:(at + j + 1) * rows]
        outs.append(acc)
        at += n
    return outs


def _dot_exact_rhs(m, v, parts):
    cols = v.shape[1]
    prod = _dot(m, jnp.concatenate(_split_bf16(v, parts), axis=1))
    acc = prod[:, 0:cols]
    for j in range(1, parts):
        acc = acc + prod[:, j * cols:(j + 1) * cols]
    return acc


def _prompt_tile(i):
    return jnp.maximum(i - 1, 0)


def _two_group_kernel(body, pre, *, n_shared, n_in, n_out, row_axis):
    def kernel(*refs):
        shared = refs[:n_shared]
        in_p = refs[n_shared:n_shared + n_in]
        in_s = refs[n_shared + n_in:n_shared + 2 * n_in]
        o0 = n_shared + 2 * n_in
        out_p = refs[o0:o0 + n_out]
        out_s = refs[o0 + n_out:o0 + 2 * n_out]
        scratch = refs[o0 + 2 * n_out:]
        if pre is not None:
            pre(shared, scratch)
        i = pl.program_id(row_axis)

        @pl.when(i == 0)
        def _():
            body(shared, in_s, out_s, scratch, 1)

        @pl.when(i > 0)
        def _():
            body(shared, in_p, out_p, scratch, 0)

    return kernel


def _two_group_call(body, pre, *, grid, row_axis, shared, ins_p, ins_s, outs_p, outs_s, scratch_shapes, vmem_mib,
                    name):
    assert len(ins_p) == len(ins_s) and len(outs_p) == len(outs_s)
    arrays = [a for a, _ in shared + ins_p + ins_s]
    in_specs = [s for _, s in shared + ins_p + ins_s]
    res = pl.pallas_call(
        _two_group_kernel(body, pre, n_shared=len(shared), n_in=len(ins_p), n_out=len(outs_p), row_axis=row_axis),
        grid=grid,
        in_specs=in_specs,
        out_specs=[s for _, s in outs_p + outs_s],
        out_shape=[a for a, _ in outs_p + outs_s],
        scratch_shapes=scratch_shapes,
        compiler_params=_cparams(("arbitrary",) * len(grid), vmem_mib),
        name=name,
    )(*arrays)
    return res[:len(outs_p)], res[len(outs_p):]


def _row_specs(tm, ts, cols, *, col_of=None, two_d_grid=True):
    if not two_d_grid:
        return (pl.BlockSpec((tm, cols), lambda i, *_: (_prompt_tile(i), 0)),
                pl.BlockSpec((ts, cols), lambda i, *_: (0, 0)))
    col_of = col_of or (lambda j: j)
    return (pl.BlockSpec((tm, cols), lambda j, i: (_prompt_tile(i), col_of(j))),
            pl.BlockSpec((ts, cols), lambda j, i: (0, col_of(j))))


def _rmsnorm_body(shared, ins, outs, scratch, group):
    x = ins[0][...]
    ms = jnp.mean(x * x, axis=-1, keepdims=True)
    outs[0][...] = (x * lax.rsqrt(ms + EPS) * shared[0][...]).astype(outs[0].dtype)


def _rmsnorm(xp, xs, g, tm):
    d = xp.shape[1]
    npt, ts = xp.shape[0] // tm, xs.shape[0]
    sp, ss = _row_specs(tm, ts, d, two_d_grid=False)
    (op,), (os_,) = _two_group_call(
        _rmsnorm_body, None, grid=(npt + 1,), row_axis=0,
        shared=[(g.reshape(1, d), pl.BlockSpec((1, d), lambda i: (0, 0)))],
        ins_p=[(xp, sp)], ins_s=[(xs, ss)],
        outs_p=[(jax.ShapeDtypeStruct(xp.shape, BF16), sp)], outs_s=[(jax.ShapeDtypeStruct(xs.shape, BF16), ss)],
        scratch_shapes=[], vmem_mib=40, name="rmsnorm")
    return op, os_


def _cast_kernel(w_ref, o_ref):
    o_ref[...] = w_ref[0].astype(o_ref.dtype)


def _cast_bf16(w, layer, *, tr):
    _, r, c = w.shape
    return pl.pallas_call(
        _cast_kernel,
        grid=(r // tr,),
        in_specs=[pl.BlockSpec((1, tr, c), lambda i: (layer, i, 0))],
        out_specs=pl.BlockSpec((tr, c), lambda i: (i, 0)),
        out_shape=jax.ShapeDtypeStruct((r, c), BF16),
        compiler_params=_cparams(("arbitrary",), 40),
        name="cast_bf16",
    )(w)


def _wspec(k, tn, layer, row0):
    assert row0 % SUBLANES == 0 and tn % SUBLANES == 0
    return pl.BlockSpec((pl.Element(1), pl.Element(tn), pl.Element(k)),
                        lambda j, i: (layer, pl.multiple_of(row0 + j * tn, SUBLANES), 0))


def _load_weight(w_ref, w16_ref, slot=None, valid_rows=None):
    @pl.when(pl.program_id(1) == 0)
    def _():
        w = w_ref[0]
        if valid_rows is not None:
            rows = lax.broadcasted_iota(jnp.int32, w.shape, 0)
            w = jnp.where(rows < valid_rows, w, 0.0)
        if slot is None:
            w16_ref[...] = w.astype(BF16)
        else:
            w16_ref[slot] = w.astype(BF16)


def _proj_outs(out_kinds, tp, ts, n, tm, tn, npt, tail_rows, tiles_per_seq):
    outs_p, outs_s = [], []
    for kind in out_kinds:
        sp, ss = _row_specs(tm, ts, tn)
        if kind == "tail":
            nseq = tp // (tm * tiles_per_seq)
            outs_p.append((jax.ShapeDtypeStruct((nseq * tail_rows, n), F32),
                           pl.BlockSpec((tail_rows, tn), lambda j, i: (_prompt_tile(i) // tiles_per_seq, j))))
            outs_s.append((jax.ShapeDtypeStruct((ts, n), F32), ss))
        else:
            dt = F32 if kind == "f32" else BF16
            outs_p.append((jax.ShapeDtypeStruct((tp, n), dt), sp))
            outs_s.append((jax.ShapeDtypeStruct((ts, n), dt), ss))
    return outs_p, outs_s


def _store_proj(y, out_kinds, outs, cols=slice(None), scale=1.0):
    for kind, o_ref in zip(out_kinds, outs):
        if kind == "tail":
            o_ref[:, cols] = y[y.shape[0] - o_ref.shape[0]:]
        elif kind == "bf16":
            o_ref[:, cols] = (y * scale).astype(BF16) if scale != 1.0 else y.astype(BF16)
        else:
            o_ref[:, cols] = y


def _proj_act(xnp, xns, wt, layer, row0, n, *, tm, tn, act, out_kinds, bias=None, valid_rows=None, tail_rows=None,
              tiles_per_seq=1, name):
    (tp, k), ts = xnp.shape, xns.shape[0]
    npt = tp // tm
    has_bias = bias is not None

    def pre(shared, scratch):
        _load_weight(shared[0], scratch[0], valid_rows=valid_rows)

    def body(shared, ins, outs, scratch, group):
        acc = _dot_nt(ins[0][...], scratch[0][...])
        if has_bias:
            acc = acc + shared[1][...]
        _store_proj(act(acc), out_kinds, outs)

    shared = [(wt, _wspec(k, tn, layer, row0))]
    if has_bias:
        shared.append((bias.reshape(1, n), pl.BlockSpec((1, tn), lambda j, i: (0, j))))
    xp_spec, xs_spec = _row_specs(tm, ts, k, col_of=lambda j: 0)
    outs_p, outs_s = _proj_outs(out_kinds, tp, ts, n, tm, tn, npt, tail_rows, tiles_per_seq)
    return _two_group_call(body, pre, grid=(n // tn, npt + 1), row_axis=1, shared=shared,
                           ins_p=[(xnp, xp_spec)], ins_s=[(xns, xs_spec)], outs_p=outs_p, outs_s=outs_s,
                           scratch_shapes=[pltpu.VMEM((tn, k), BF16)], vmem_mib=56, name=name)


def _proj_headnorm(xnp, xns, wt, layer, row0, n, g, *, tm, tn, scale, out_kinds, tail_rows=None, tiles_per_seq=1,
                   name):
    (tp, k), ts = xnp.shape, xns.shape[0]
    npt = tp // tm

    def pre(shared, scratch):
        _load_weight(shared[0], scratch[0])

    def body(shared, ins, outs, scratch, group):
        acc = _dot_nt(ins[0][...], scratch[0][...])
        gain = shared[1][...]
        for h in range(tn // HEAD_DIM):
            sl = slice(h * HEAD_DIM, (h + 1) * HEAD_DIM)
            blk = acc[:, sl]
            ms = jnp.mean(blk * blk, axis=-1, keepdims=True)
            _store_proj(blk * lax.rsqrt(ms + EPS) * gain, out_kinds, outs, cols=sl, scale=scale)

    shared = [(wt, _wspec(k, tn, layer, row0)),
              (g.reshape(1, HEAD_DIM), pl.BlockSpec((1, HEAD_DIM), lambda j, i: (0, 0)))]
    xp_spec, xs_spec = _row_specs(tm, ts, k, col_of=lambda j: 0)
    outs_p, outs_s = _proj_outs(out_kinds, tp, ts, n, tm, tn, npt, tail_rows, tiles_per_seq)
    return _two_group_call(body, pre, grid=(n // tn, npt + 1), row_axis=1, shared=shared,
                           ins_p=[(xnp, xp_spec)], ins_s=[(xns, xs_spec)], outs_p=outs_p, outs_s=outs_s,
                           scratch_shapes=[pltpu.VMEM((tn, k), BF16)], vmem_mib=56, name=name)


CONV_SUB = 256


def _causal_conv(u, cs, cw_ref, p_ref, carry_ref, st_ref, *, width, nseg, tiles_per_seq):
    tm, tn = u.shape
    seg_len = tm // nseg
    row8 = lax.broadcasted_iota(jnp.int32, (SUBLANES, tn), 0)
    outs = []
    for s in range(nseg):
        seg = u[s * seg_len:(s + 1) * seg_len]
        prev8 = carry_ref[:, cs] if tiles_per_seq > 1 else p_ref[s, :, cs]
        acc = cw_ref[width - 1:width, cs] * seg
        for k in range(1, width):
            sh = pltpu.roll(seg, k, 0)
            first8 = jnp.where(row8 < k, pltpu.roll(prev8, k, 0), sh[0:SUBLANES])
            shk = jnp.concatenate([first8, sh[SUBLANES:]], axis=0)
            acc = acc + cw_ref[width - 1 - k:width - k, cs] * shk
        outs.append(acc)
        st_ref[s, :, cs] = seg[seg_len - SUBLANES:seg_len]
    if tiles_per_seq > 1:
        carry_ref[:, cs] = u[tm - SUBLANES:tm]
    return outs[0] if nseg == 1 else jnp.concatenate(outs, axis=0)


def _proj_conv(kind, xnp, xns, wt, layer, row0s, n, conv_w8, conv_b, prefix_p, prefix_s, *, tm, tn, seq_p, seq_s,
               out_dtype, name):
    (tp, k), ts = xnp.shape, xns.shape[0]
    npt = tp // tm
    tps_p = seq_p // tm
    nseg_s = ts // seq_s
    assert tps_p >= 1 and seq_p % tm == 0 and ts % seq_s == 0
    width = SSM_CONV if kind == "xbc" else SC_WIDTH
    nw = len(row0s)
    has_b = conv_b is not None

    def pre(shared, scratch):
        for slot in range(nw):
            _load_weight(shared[slot], scratch[1], slot=slot)

    def body(shared, ins, outs, scratch, group):
        x_ref, p_ref = ins
        o_ref, st_ref = outs
        carry_ref, w16_ref = scratch
        cw_ref = shared[nw]
        nseg, tps = (1, tps_p) if group == 0 else (nseg_s, 1)
        if tps > 1:
            @pl.when(lax.rem(_prompt_tile(pl.program_id(1)), tps) == 0)
            def _():
                carry_ref[...] = p_ref[0]
        x = x_ref[...]
        for c in range(tn // CONV_SUB):
            cs = slice(c * CONV_SUB, (c + 1) * CONV_SUB)
            if kind == "xbc":
                u = _dot_nt(x, w16_ref[0, cs, :])
            else:
                u = _dot_nt(x, w16_ref[1, cs, :]) * _dot_nt(x, w16_ref[2, cs, :])
            y = _causal_conv(u, cs, cw_ref, p_ref, carry_ref, st_ref, width=width, nseg=nseg, tiles_per_seq=tps)
            if kind == "xbc":
                y = y + shared[nw + 1][:, cs]
                o_ref[:, cs] = y * jax.nn.sigmoid(y)
            else:
                o_ref[:, cs] = (_dot_nt(x, w16_ref[0, cs, :]) * y).astype(o_ref.dtype)

    shared = [(wt, _wspec(k, tn, layer, r)) for r in row0s]
    shared.append((conv_w8, pl.BlockSpec((SUBLANES, tn), lambda j, i: (0, j))))
    if has_b:
        shared.append((conv_b.reshape(1, n), pl.BlockSpec((1, tn), lambda j, i: (0, j))))
    xp_spec, xs_spec = _row_specs(tm, ts, k, col_of=lambda j: 0)
    yp_spec, ys_spec = _row_specs(tm, ts, tn)
    ins_p = [(xnp, xp_spec),
             (prefix_p, pl.BlockSpec((1, SUBLANES, tn), lambda j, i: (_prompt_tile(i) // tps_p, 0, j)))]
    ins_s = [(xns, xs_spec), (prefix_s, pl.BlockSpec((nseg_s, SUBLANES, tn), lambda j, i: (0, 0, j)))]
    outs_p = [(jax.ShapeDtypeStruct((tp, n), out_dtype), yp_spec),
              (jax.ShapeDtypeStruct((npt, SUBLANES, n), F32),
               pl.BlockSpec((1, SUBLANES, tn), lambda j, i: (_prompt_tile(i), 0, j)))]
    outs_s = [(jax.ShapeDtypeStruct((ts, n), out_dtype), ys_spec),
              (jax.ShapeDtypeStruct((nseg_s, SUBLANES, n), F32),
               pl.BlockSpec((nseg_s, SUBLANES, tn), lambda j, i: (0, 0, j)))]
    (yp, tails_p), (ys, tails_s) = _two_group_call(
        body, pre, grid=(n // tn, npt + 1), row_axis=1, shared=shared, ins_p=ins_p, ins_s=ins_s, outs_p=outs_p,
        outs_s=outs_s, scratch_shapes=[pltpu.VMEM((SUBLANES, tn), F32), pltpu.VMEM((nw, tn, k), BF16)],
        vmem_mib=56, name=name)
    return (yp, tails_p[tps_p - 1::tps_p]), (ys, tails_s)


def _ssd_kernel(x_ref, b_ref, c_ref, dt_ref, zs_ref, alog_ref, dx_ref, ng_ref, ech_ref, els_ref, h0_ref,
                y_ref, hl_ref, ht_ref, *, L, cps, nsteps):
    ci = pl.program_id(1)
    hp = LANES // L
    ntiles = SSM_HEADS // hp
    gw = SSM_GROUP_DIM
    tw = hp * SSM_HEAD_DIM
    log2_l = L.bit_length() - 1

    @pl.when(ci == 0)
    def _():
        ht_ref[...] = h0_ref[0].T

    a = -jnp.exp(alog_ref[...])
    ri = lax.broadcasted_iota(jnp.int32, (L, L), 0)
    cj = lax.broadcasted_iota(jnp.int32, (L, L), 1)
    tri = (ri >= cj).astype(BF16)
    lane_blk = jnp.right_shift(lax.broadcasted_iota(jnp.int32, (1, LANES), 1), log2_l)
    row_l = lax.broadcasted_iota(jnp.int32, (L, LANES), 0)
    lane_s = jnp.bitwise_and(lax.broadcasted_iota(jnp.int32, (L, LANES), 1), L - 1)
    causal = row_l >= lane_s
    rb = jnp.right_shift(lax.broadcasted_iota(jnp.int32, (LANES, tw), 0), log2_l)
    cb_ = jnp.right_shift(lax.broadcasted_iota(jnp.int32, (LANES, tw), 1), SSM_HEAD_DIM.bit_length() - 1)
    blockdiag = rb == cb_
    tile_rows = lambda v: jnp.concatenate([v] * hp, axis=0)
    zpad = jnp.zeros((LANES - L, gw), F32)

    for r in range(cps):
        rows = slice(r * L, (r + 1) * L)
        x = x_ref[rows, :]
        bm = b_ref[rows, :]
        cm = c_ref[rows, :]
        dt = dt_ref[rows, :]
        da = dt * a

        acum = _dot_exact_rhs(tri, da, 3)
        eacum = jnp.exp(acum)
        dend = jnp.exp(acum[L - 1:L, :] - acum)
        w = dt * dend

        ech = ech_ref[...]
        if L == SSM_HEAD_DIM:
            cexp = jnp.concatenate(
                [jnp.broadcast_to(acum[:, h:h + 1], (L, SSM_HEAD_DIM)) for h in range(SSM_HEADS)], axis=1)
        else:
            (cexp,) = _dot_exact_lhs([acum], els_ref[...], [3])
        wx, ex = _dot_exact_lhs([w, eacum], ech, [2, 2])

        acum_t = tile_rows(acum).T
        dt_t = tile_rows(dt).T

        bsq = [tile_rows(bm[:, g * SSM_STATE:(g + 1) * SSM_STATE]) for g in range(SSM_GROUPS)]
        cbt = [_dot_nt(cm[:, g * SSM_STATE:(g + 1) * SSM_STATE].astype(BF16), bsq[g].astype(BF16))
               for g in range(SSM_GROUPS)]

        yd = []
        for t in range(ntiles):
            h_first = t * hp
            g = h_first // (SSM_HEADS // SSM_GROUPS)
            r_row = acum_t[h_first:h_first + 1, :]
            d_row = dt_t[h_first:h_first + 1, :]
            for jj in range(1, hp):
                sel = lane_blk == jj
                r_row = jnp.where(sel, acum_t[h_first + jj:h_first + jj + 1, :], r_row)
                d_row = jnp.where(sel, dt_t[h_first + jj:h_first + jj + 1, :], d_row)
            diff = cexp[:, t * LANES:(t + 1) * LANES] - r_row
            dec = jnp.exp(jnp.where(causal, diff, -jnp.inf))
            sc = (cbt[g] * dec * d_row).astype(BF16)
            xs = tile_rows(x[:, t * tw:(t + 1) * tw])
            rhs = jnp.where(blockdiag, xs, 0.0).astype(BF16)
            yd.append(_dot(sc, rhs))
        y = jnp.concatenate(yd, axis=1)

        xw = x * wx
        for g in range(SSM_GROUPS):
            gs = slice(g * gw, (g + 1) * gw)
            h_in = ht_ref[:, gs]
            y_off = _dot(cm[:, g * SSM_STATE:(g + 1) * SSM_STATE].astype(BF16), h_in.astype(BF16))
            yg = y[:, gs] + y_off * ex[:, gs] + dx_ref[:, gs] * x[:, gs]
            yg = yg * zs_ref[rows, gs]
            ms = jnp.mean(yg * yg, axis=-1, keepdims=True)
            y_ref[rows, gs] = (yg * lax.rsqrt(ms + EPS) * ng_ref[:, gs]).astype(y_ref.dtype)
            bm_t = bsq[g].T.astype(BF16)
            upd = jnp.concatenate([xw[:, gs], zpad], axis=0).astype(BF16)
            ht_ref[:, gs] = h_in * ex[L - 1:L, gs] + _dot(bm_t, upd)

    @pl.when(ci == nsteps - 1)
    def _():
        hl_ref[0] = ht_ref[...].T


def _ssd(xbc_act, dt, zs, a_log128, d_x, norm_g, ech, els, h0, *, L, cps, seq_len):
    t = xbc_act.shape[0]
    nseq = h0.shape[0]
    rows = L * cps
    ns = seq_len // rows
    rmap = lambda b, c: (b * ns + c, 0)
    cmap = lambda b, c: (0, 0)
    nb = SSM_INNER // SSM_BC
    return pl.pallas_call(
        functools.partial(_ssd_kernel, L=L, cps=cps, nsteps=ns),
        grid=(nseq, ns),
        in_specs=[pl.BlockSpec((rows, SSM_INNER), rmap),
                  pl.BlockSpec((rows, SSM_BC), lambda b, c: (b * ns + c, nb)),
                  pl.BlockSpec((rows, SSM_BC), lambda b, c: (b * ns + c, nb + 1)),
                  pl.BlockSpec((rows, LANES), rmap),
                  pl.BlockSpec((rows, SSM_INNER), rmap),
                  pl.BlockSpec((1, LANES), cmap),
                  pl.BlockSpec((1, SSM_INNER), cmap),
                  pl.BlockSpec((1, SSM_INNER), cmap),
                  pl.BlockSpec(ech.shape, cmap),
                  pl.BlockSpec(els.shape, cmap),
                  pl.BlockSpec((1, SSM_INNER, SSM_STATE), lambda b, c: (b, 0, 0))],
        out_specs=[pl.BlockSpec((rows, SSM_INNER), rmap),
                   pl.BlockSpec((1, SSM_INNER, SSM_STATE), lambda b, c: (b, 0, 0))],
        out_shape=[jax.ShapeDtypeStruct((t, SSM_INNER), BF16),
                   jax.ShapeDtypeStruct((nseq, SSM_INNER, SSM_STATE), F32)],
        scratch_shapes=[pltpu.VMEM((SSM_STATE, SSM_INNER), F32)],
        compiler_params=_cparams(("arbitrary", "arbitrary"), 48),
        name=f"ssd_L{L}",
    )(xbc_act, xbc_act, xbc_act, dt, zs, a_log128, d_x, norm_g, ech, els, h0)


def _softmax_pv(s_parts, v_parts):
    m = functools.reduce(jnp.maximum, [jnp.max(s, axis=1, keepdims=True) for s in s_parts])
    l = None
    o = None
    for s, v in zip(s_parts, v_parts):
        p = jnp.exp(s - m)
        ls = jnp.sum(p, axis=1, keepdims=True)
        os_ = _dot(p.astype(BF16), v)
        l = ls if l is None else l + ls
        o = os_ if o is None else o + os_
    return o / l


def _attn_prompt_kernel(q_ref, k0_ref, k1_ref, k2_ref, v0_ref, v1_ref, v2_ref, bias_ref, o_ref):
    i = pl.program_id(1)
    k_refs = (k0_ref, k1_ref, k2_ref)
    v_refs = (v0_ref, v1_ref, v2_ref)

    first_valid = (ATT_K_BLOCKS - 1 - i) * ATT_Q_BLOCK
    kidx = lax.broadcasted_iota(jnp.int32, (ATT_Q_BLOCK, ATT_Q_BLOCK), 1)
    for h in range(N_HEADS):
        sl = slice(h * HEAD_DIM, (h + 1) * HEAD_DIM)
        q = q_ref[:, sl]
        s_parts = []
        for kb in range(ATT_K_BLOCKS):
            s = _dot_nt(q, k_refs[kb][:, sl]) + bias_ref[h, :, kb * ATT_Q_BLOCK:(kb + 1) * ATT_Q_BLOCK]
            s_parts.append(jnp.where(kidx + kb * ATT_Q_BLOCK >= first_valid, s, NEG_INF))
        o = _softmax_pv(s_parts, [v_refs[kb][:, sl] for kb in range(ATT_K_BLOCKS)])
        o_ref[:, sl] = o.astype(o_ref.dtype)


def _attn_prompt(q, k, v, bias, *, nseq, seq_len):
    t = q.shape[0]
    nqb = seq_len // ATT_Q_BLOCK
    qmap = lambda b, i: (b * nqb + i, 0)

    def kmap(back):
        return lambda b, i: (b * nqb + jnp.maximum(i - back, 0), 0)

    blk = (ATT_Q_BLOCK, ATT_DIM)
    kv_specs = [pl.BlockSpec(blk, kmap(ATT_K_BLOCKS - 1 - kb)) for kb in range(ATT_K_BLOCKS)]
    return pl.pallas_call(
        _attn_prompt_kernel,
        grid=(nseq, nqb),
        in_specs=[pl.BlockSpec(blk, qmap)] + kv_specs + kv_specs
        + [pl.BlockSpec(bias.shape, lambda b, i: (0, 0, 0))],
        out_specs=pl.BlockSpec(blk, qmap),
        out_shape=jax.ShapeDtypeStruct((t, ATT_DIM), BF16),
        compiler_params=_cparams(("arbitrary", "arbitrary"), 56),
        name="attn_prompt",
    )(q, k, k, k, v, v, v, bias)


def _heads_to_rows(ref):
    return jnp.concatenate([ref[:, h * HEAD_DIM:(h + 1) * HEAD_DIM] for h in range(N_HEADS)], axis=0)


def _attn_sample_kernel(q_ref, kn_ref, vn_ref, kc_ref, vc_ref, bf_ref, bc_ref, bn_ref, o_ref):
    t = q_ref.shape[0]
    q2 = _heads_to_rows(q_ref)
    kn2, vn2 = _heads_to_rows(kn_ref), _heads_to_rows(vn_ref)
    npos = SAMPLE_KEY_TILE // N_HEADS
    ntiles = kc_ref.shape[2] // npos
    nfar = ntiles - bc_ref.shape[1] // SAMPLE_KEY_TILE
    assert t & (t - 1) == 0
    row_head = jnp.right_shift(lax.broadcasted_iota(jnp.int32, (N_HEADS * t, SAMPLE_KEY_TILE), 0),
                               t.bit_length() - 1)
    col_head = jnp.bitwise_and(lax.broadcasted_iota(jnp.int32, (N_HEADS * t, SAMPLE_KEY_TILE), 1), N_HEADS - 1)
    far_bias = jnp.where(row_head == col_head, bf_ref[...], NEG_INF)
    s_parts, v_parts = [], []
    for c in range(ntiles):
        ps = slice(c * npos, (c + 1) * npos)
        kc = kc_ref[0, 0, ps, :, :].reshape(SAMPLE_KEY_TILE, HEAD_DIM)
        vc = vc_ref[0, 0, ps, :, :].reshape(SAMPLE_KEY_TILE, HEAD_DIM)
        if c < nfar:
            bias = far_bias
        else:
            bias = bc_ref[:, (c - nfar) * SAMPLE_KEY_TILE:(c - nfar + 1) * SAMPLE_KEY_TILE]
        s_parts.append(_dot_nt(q2, kc.astype(BF16)) + bias)
        v_parts.append(vc.astype(BF16))
    s_parts.append(_dot_nt(q2, kn2) + bn_ref[...])
    v_parts.append(vn2)
    o2 = _softmax_pv(s_parts, v_parts)
    for h in range(N_HEADS):
        o_ref[:, h * HEAD_DIM:(h + 1) * HEAD_DIM] = o2[h * t:(h + 1) * t].astype(o_ref.dtype)


def _attn_sample(q, k, v, k_cache, v_cache, layer, bias_far, bias_c, bias_n, *, nseq, seq_len):
    t = q.shape[0]
    blk = (seq_len, ATT_DIM)
    rmap = lambda b: (b, 0)
    const = lambda b: (0, 0)
    cspec = pl.BlockSpec((1, 1) + k_cache.shape[2:], lambda b: (layer, b, 0, 0, 0))
    return pl.pallas_call(
        _attn_sample_kernel,
        grid=(nseq,),
        in_specs=[pl.BlockSpec(blk, rmap), pl.BlockSpec(blk, rmap), pl.BlockSpec(blk, rmap), cspec, cspec,
                  pl.BlockSpec(bias_far.shape, const), pl.BlockSpec(bias_c.shape, const),
                  pl.BlockSpec(bias_n.shape, const)],
        out_specs=pl.BlockSpec(blk, rmap),
        out_shape=jax.ShapeDtypeStruct((t, ATT_DIM), BF16),
        compiler_params=_cparams(("arbitrary",), 56),
        name="attn_sample",
    )(q, k, v, k_cache, v_cache, bias_far, bias_c, bias_n)


def _merge(ys_p, ys_s, w_ssm, w_sc, w_att, layer, gates_p, gates_s, *, tm, tn):
    (tp, k), ts = ys_p[0].shape, ys_s[0].shape[0]
    n = w_ssm.shape[2]
    ncol, npt = n // tn, tp // tm

    def pre(shared, scratch):
        @pl.when(pl.program_id(1) == 0)
        def _():
            for b in range(N_BRANCH):
                scratch[0][b] = shared[b][0].astype(BF16)

    def body(shared, ins, outs, scratch, group):
        w16_ref = scratch[0]
        m = ins[3][...] * _dot(ins[0][...], w16_ref[0])
        m = m + ins[4][...] * _dot(ins[1][...], w16_ref[1])
        m = m + ins[5][...] * _dot(ins[2][...], w16_ref[2])
        outs[0][...] = m.astype(outs[0].dtype)

    rhs = pl.BlockSpec((1, k, tn), lambda j, i: (layer, 0, j))
    lp, ls = _row_specs(tm, ts, k, col_of=lambda j: 0)
    ins_p = [(y, lp) for y in ys_p]
    ins_s = [(y, ls) for y in ys_s]
    for b in range(N_BRANCH):
        gp, gs = _row_specs(tm, ts, tn, col_of=lambda j, b=b: b * ncol + j)
        ins_p.append((gates_p, gp))
        ins_s.append((gates_s, gs))
    op, os_ = _row_specs(tm, ts, tn)
    (mp,), (ms,) = _two_group_call(
        body, pre, grid=(ncol, npt + 1), row_axis=1, shared=[(w_ssm, rhs), (w_sc, rhs), (w_att, rhs)],
        ins_p=ins_p, ins_s=ins_s, outs_p=[(jax.ShapeDtypeStruct((tp, n), BF16), op)],
        outs_s=[(jax.ShapeDtypeStruct((ts, n), BF16), os_)],
        scratch_shapes=[pltpu.VMEM((N_BRANCH, k, tn), BF16)], vmem_mib=56, name="merge")
    return mp, ms


def _wo_body(shared, ins, outs, scratch, group):
    h = ins[0][...] + _dot(ins[1][...], shared[0][...])
    outs[0][...] = h
    ms = jnp.mean(h * h, axis=-1, keepdims=True)
    outs[1][...] = (h * lax.rsqrt(ms + EPS) * shared[1][...]).astype(outs[1].dtype)


def _wo(xp, xs, mp, ms, w_o, g, *, tm):
    (tp, d), ts = xp.shape, xs.shape[0]
    npt = tp // tm
    sp, ss = _row_specs(tm, ts, d, two_d_grid=False)
    const = lambda i: (0, 0)
    (hp, hnp), (hs, hns) = _two_group_call(
        _wo_body, None, grid=(npt + 1,), row_axis=0,
        shared=[(w_o, pl.BlockSpec((d, d), const)), (g.reshape(1, d), pl.BlockSpec((1, d), const))],
        ins_p=[(xp, sp), (mp, sp)], ins_s=[(xs, ss), (ms, ss)],
        outs_p=[(jax.ShapeDtypeStruct((tp, d), F32), sp), (jax.ShapeDtypeStruct((tp, d), BF16), sp)],
        outs_s=[(jax.ShapeDtypeStruct((ts, d), F32), ss), (jax.ShapeDtypeStruct((ts, d), BF16), ss)],
        scratch_shapes=[], vmem_mib=52, name="wo")
    return (hp, hnp), (hs, hns)


def _ffn_body(shared, ins, outs, scratch, group):
    c = pl.program_id(1)
    a = _dot(ins[1][...], shared[0][...])
    a = jnp.square(jnp.maximum(a, 0.0)).astype(BF16)
    contrib = _dot(a, shared[1][...])

    @pl.when(c == 0)
    def _():
        outs[0][...] = ins[0][...] + contrib

    @pl.when(c > 0)
    def _():
        outs[0][...] += contrib


def _ffn(hp, hnp, hs, hns, w1, w2, *, tm, tc):
    (tp, d), ts = hp.shape, hs.shape[0]
    dff = w1.shape[1]
    npt = tp // tm
    sp, ss = _row_specs(tm, ts, d, two_d_grid=False)
    (yp,), (ys,) = _two_group_call(
        _ffn_body, None, grid=(npt + 1, dff // tc), row_axis=0,
        shared=[(w1, pl.BlockSpec((d, tc), lambda i, c: (0, c))), (w2, pl.BlockSpec((tc, d), lambda i, c: (c, 0)))],
        ins_p=[(hp, sp), (hnp, sp)], ins_s=[(hs, ss), (hns, ss)],
        outs_p=[(jax.ShapeDtypeStruct((tp, d), F32), sp)], outs_s=[(jax.ShapeDtypeStruct((ts, d), F32), ss)],
        scratch_shapes=[], vmem_mib=52, name="ffn")
    return yp, ys


def _head_expand(lanes_per_head):
    rows = lax.broadcasted_iota(jnp.int32, (LANES, SSM_HEADS * lanes_per_head), 0)
    cols = lax.broadcasted_iota(jnp.int32, (LANES, SSM_HEADS * lanes_per_head), 1) // lanes_per_head
    return (rows == cols).astype(BF16)


def _bias_tables_kernel(row_ref, toe_ref, pb_ref):
    x = jnp.broadcast_to(row_ref[0], (ATT_Q_BLOCK, TOEPLITZ_COLS))
    t = pltpu.roll(x, 0, 1, stride=1, stride_axis=0)
    toe_ref[0] = t[:toe_ref.shape[1]]
    qi = lax.broadcasted_iota(jnp.int32, (ATT_Q_BLOCK, ATT_WINDOW), 0)
    kj = lax.broadcasted_iota(jnp.int32, (ATT_Q_BLOCK, ATT_WINDOW), 1)
    band = kj - jnp.bitwise_and(qi, -CHUNK)
    pb_ref[0] = jnp.where((band >= 0) & (band < ATT_BAND), t[:, :ATT_WINDOW], NEG_INF)


def _bias_tables(rel_bias, sample_rows):
    tbl = rel_bias.astype(F32)
    far = jnp.broadcast_to(tbl[:, 2 * MAX_REL:], (N_HEADS, ATT_PAST - MAX_REL))
    wrap = jnp.broadcast_to(tbl[:, 2 * MAX_REL:], (N_HEADS, ATT_Q_BLOCK - 1))
    n_near = TOEPLITZ_COLS - far.shape[1] - tbl.shape[1] - wrap.shape[1]
    assert ATT_WINDOW <= far.shape[1] + tbl.shape[1] + n_near
    near = jnp.broadcast_to(tbl[:, :1], (N_HEADS, n_near))
    row = jnp.concatenate([far, tbl[:, ::-1], near, wrap], axis=1).reshape(N_HEADS, 1, TOEPLITZ_COLS)
    return pl.pallas_call(
        _bias_tables_kernel,
        grid=(N_HEADS,),
        in_specs=[pl.BlockSpec((1, 1, TOEPLITZ_COLS), lambda h: (h, 0, 0))],
        out_specs=[pl.BlockSpec((1, sample_rows, TOEPLITZ_COLS), lambda h: (h, 0, 0)),
                   pl.BlockSpec((1, ATT_Q_BLOCK, ATT_WINDOW), lambda h: (h, 0, 0))],
        out_shape=[jax.ShapeDtypeStruct((N_HEADS, sample_rows, TOEPLITZ_COLS), F32),
                   jax.ShapeDtypeStruct((N_HEADS, ATT_Q_BLOCK, ATT_WINDOW), F32)],
        compiler_params=_cparams(("arbitrary",), 32),
        name="bias_tables",
    )(row)


def _sample_bias(toeplitz, t, lc):
    npos = SAMPLE_KEY_TILE // N_HEADS
    far = (lc - MAX_REL) // npos * npos
    same = jnp.arange(N_HEADS)[:, None] == jnp.arange(N_HEADS)[None, :]
    bias_far = jnp.broadcast_to(toeplitz[:, :1, :1], (N_HEADS, t, 1))
    tc = toeplitz[:, :t, far:lc]
    bias_c = jnp.where(same[:, None, None, :], tc[:, :, :, None], NEG_INF)
    tn_ = toeplitz[:, :t, lc:lc + t]
    bias_n = jnp.where(same[:, None, :, None], tn_[:, :, None, :], NEG_INF)
    return (bias_far.reshape(N_HEADS * t, 1), bias_c.reshape(N_HEADS * t, (lc - far) * N_HEADS),
            bias_n.reshape(N_HEADS * t, N_HEADS * t))


def _pad_rows_to8(a, axis):
    pad = [(0, 0)] * a.ndim
    pad[axis] = (SUBLANES - a.shape[axis], 0)
    return jnp.pad(a, pad)


def _layer(xp, xs, lw, layer, gp, gs):
    tm, tn = PROMPT_ROW_TILE, 1024
    wt = lw["w_in_t"]
    off = IN_OFFSETS
    seq_p, seq_s = gp["seq_len"], gs["seq_len"]
    tiles_per_seq = seq_p // tm
    tail_rows = min(tm, ATT_PAST, seq_p)

    xnp, xns = _rmsnorm(xp, xs, lw["norm_mix_g"], tm)
    proj = functools.partial(_proj_act, xnp, xns, wt, layer, tm=tm)
    (zp,), (zs,) = proj(off[0], SSM_INNER, tn=tn, act=lambda a: a * jax.nn.sigmoid(a), out_kinds=["f32"],
                        name="proj_z")
    (dtp,), (dts,) = proj(off[2], LANES, tn=LANES, act=jax.nn.softplus, out_kinds=["f32"], bias=lw["dt_bias128"],
                          valid_rows=SSM_HEADS, name="proj_dt")
    (gtp,), (gts,) = proj(off[9], N_BRANCH * D_MODEL, tn=tn, act=jax.nn.sigmoid, out_kinds=["f32"],
                          name="proj_gates")
    (v16p, vtp), (v16s, vts) = proj(off[8], ATT_DIM, tn=tn, act=lambda a: a, out_kinds=["bf16", "tail"],
                                    tail_rows=tail_rows, tiles_per_seq=tiles_per_seq, name="proj_v")
    (q16p,), (q16s,) = _proj_headnorm(xnp, xns, wt, layer, off[6], ATT_DIM, lw["q_norm_g"], tm=tm, tn=tn,
                                      scale=HEAD_DIM ** -0.5, out_kinds=["bf16"], name="proj_q")
    (k16p, ktp), (k16s, kts) = _proj_headnorm(xnp, xns, wt, layer, off[7], ATT_DIM, lw["k_norm_g"], tm=tm, tn=tn,
                                              scale=1.0, out_kinds=["bf16", "tail"], tail_rows=tail_rows,
                                              tiles_per_seq=tiles_per_seq, name="proj_k")
    (xbcp, convp), (xbcs, convs) = _proj_conv(
        "xbc", xnp, xns, wt, layer, [off[1]], SSM_CONV_DIM, lw["ssm_conv_w8"], lw["ssm_conv_b"],
        gp["ssm_conv_prefix"], gs["ssm_conv_prefix"], tm=tm, tn=512, seq_p=seq_p, seq_s=seq_s, out_dtype=F32,
        name="proj_xbc")
    (yscp, scp), (yscs, scs) = _proj_conv(
        "sc", xnp, xns, wt, layer, [off[3], off[4], off[5]], SC_DIM, lw["sc_conv_w8"], None,
        gp["sc_prefix"], gs["sc_prefix"], tm=tm, tn=512, seq_p=seq_p, seq_s=seq_s, out_dtype=BF16, name="proj_sc")

    ssd = functools.partial(_ssd, a_log128=lw["a_log128"], d_x=lw["d_x"], norm_g=lw["ssm_norm_g"], ech=lw["ech"])
    yssmp, hlp = ssd(xbcp, dtp, zp, els=lw["ech"], h0=gp["ssm_h0"], L=CHUNK,
                     cps=min(SSD_CHUNKS_PER_STEP, seq_p // CHUNK), seq_len=seq_p)
    yssms, hls = ssd(xbcs, dts, zs, els=_head_expand(seq_s), h0=gs["ssm_h0"], L=seq_s, cps=1, seq_len=seq_s)

    op = _attn_prompt(q16p, k16p, v16p, lw["prompt_bias"], nseq=gp["nseq"], seq_len=seq_p)
    os_ = _attn_sample(q16s, k16s, v16s, gs["kv_cache"][0], gs["kv_cache"][1], layer, *lw["sample_bias"],
                       nseq=gs["nseq"], seq_len=seq_s)

    mtm = 512
    mp, ms = _merge((yssmp, yscp, op), (yssms, yscs, os_), lw["ssm_out_w"], lw["sc_out_w"], lw["attn_out_w"],
                    layer, gtp, gts, tm=mtm, tn=512)
    (hp, hnp), (hs, hns) = _wo(xp, xs, mp, ms, lw["w_o"], lw["norm_ffn_g"], tm=mtm)
    yp, ys = _ffn(hp, hnp, hs, hns, lw["ffn_w1"], lw["ffn_w2"], tm=mtm, tc=1024)
    return (yp, (ktp, vtp, hlp, convp, scp)), (ys, (kts, vts, hls, convs, scs))


def kernel(x_prompt, x_sample, cache_attn_k, cache_attn_v, state_ssm, state_ssm_conv, state_short_conv,
           norm_mix_g, w_in, ssm_conv_w, ssm_conv_b, ssm_dt_bias, ssm_a_log, ssm_d, ssm_norm_g, ssm_out_w,
           sc_conv_w, sc_out_w, q_norm_g, k_norm_g, rel_bias, attn_out_w, w_o, norm_ffn_g, ffn_w1, ffn_w2):
    bp, lp, d = x_prompt.shape
    bs, ls, _ = x_sample.shape
    lc = cache_attn_k.shape[2]
    assert lc == ATT_PAST and lc + ls <= TOEPLITZ_COLS and ls <= ATT_Q_BLOCK
    ech = _head_expand(SSM_HEAD_DIM)
    w_in_t = jnp.swapaxes(w_in, 1, 2)

    yp = x_prompt.reshape(bp * lp, d)
    ys = x_sample.reshape(bs * ls, d)
    new_p, new_s = [], []
    for l in range(DEPTH):
        toeplitz, prompt_bias = _bias_tables(rel_bias[l], ls)
        lw = {
            "norm_mix_g": norm_mix_g[l], "norm_ffn_g": norm_ffn_g[l],
            "w_in_t": w_in_t,
            "dt_bias128": jnp.pad(ssm_dt_bias[l].astype(F32), (0, LANES - SSM_HEADS)),
            "a_log128": jnp.pad(ssm_a_log[l].astype(F32), (0, LANES - SSM_HEADS)).reshape(1, LANES),
            "d_x": jnp.repeat(ssm_d[l].astype(F32), SSM_HEAD_DIM).reshape(1, SSM_INNER),
            "ssm_norm_g": ssm_norm_g[l].astype(F32).reshape(1, SSM_INNER),
            "ssm_conv_w8": jnp.pad(ssm_conv_w[l].astype(F32), ((0, SUBLANES - SSM_CONV), (0, 0))),
            "ssm_conv_b": ssm_conv_b[l].astype(F32),
            "sc_conv_w8": jnp.pad(sc_conv_w[l].astype(F32), ((0, SUBLANES - SC_WIDTH), (0, 0))),
            "q_norm_g": q_norm_g[l].astype(F32), "k_norm_g": k_norm_g[l].astype(F32),
            "ssm_out_w": ssm_out_w, "sc_out_w": sc_out_w, "attn_out_w": attn_out_w,
            "w_o": _cast_bf16(w_o, l, tr=1024),
            "ffn_w1": _cast_bf16(ffn_w1, l, tr=256), "ffn_w2": _cast_bf16(ffn_w2, l, tr=1024),
            "ech": ech,
            "prompt_bias": prompt_bias,
            "sample_bias": _sample_bias(toeplitz, ls, lc),
        }
        gp = dict(nseq=bp, seq_len=lp,
                  ssm_conv_prefix=jnp.zeros((bp, SUBLANES, SSM_CONV_DIM), F32),
                  ssm_h0=jnp.zeros((bp, SSM_INNER, SSM_STATE), F32),
                  sc_prefix=jnp.zeros((bp, SUBLANES, SC_DIM), F32))
        gs = dict(nseq=bs, seq_len=ls,
                  ssm_conv_prefix=_pad_rows_to8(state_ssm_conv[l].astype(F32), 1),
                  ssm_h0=state_ssm[l].astype(F32).reshape(bs, SSM_INNER, SSM_STATE),
                  sc_prefix=_pad_rows_to8(state_short_conv[l].astype(F32), 1),
                  kv_cache=(cache_attn_k, cache_attn_v))
        (yp, st_p), (ys, st_s) = _layer(yp, ys, lw, l, gp, gs)
        new_p.append(st_p)
        new_s.append(st_s)

    keep = min(ATT_PAST, lp)

    def stack(states, fn):
        return jnp.stack([fn(s) for s in states])

    hshape = lambda b: (b, SSM_HEADS, SSM_HEAD_DIM, SSM_STATE)
    return (
        yp.reshape(bp, lp, d),
        ys.reshape(bs, ls, d),
        stack(new_p, lambda s: s[0].reshape(bp, keep, N_HEADS, HEAD_DIM)),
        stack(new_p, lambda s: s[1].reshape(bp, keep, N_HEADS, HEAD_DIM)),
        stack(new_s, lambda s: s[0].reshape(bs, ls, N_HEADS, HEAD_DIM)),
        stack(new_s, lambda s: s[1].reshape(bs, ls, N_HEADS, HEAD_DIM)),
        stack(new_p, lambda s: s[2].reshape(hshape(bp))),
        stack(new_s, lambda s: s[2].reshape(hshape(bs))),
        stack(new_p, lambda s: s[3][:, SUBLANES - (SSM_CONV - 1):]),
        stack(new_s, lambda s: s[3][:, SUBLANES - (SSM_CONV - 1):]),
        stack(new_p, lambda s: s[4][:, SUBLANES - (SC_WIDTH - 1):]),
        stack(new_s, lambda s: s[4][:, SUBLANES - (SC_WIDTH - 1):]),
    )
```

```python
import functools
import math

import jax
import jax.numpy as jnp
from jax import lax
from jax.experimental import pallas as pl
from jax.experimental.pallas import tpu as pltpu

F32 = jnp.float32
BF16 = jnp.bfloat16

D_MODEL = 2048
DEPTH = 2
CHUNK = 64
EPS = 1e-6

SSM_INNER = D_MODEL
SSM_HEAD_DIM = 64
SSM_HEADS = SSM_INNER // SSM_HEAD_DIM
SSM_GROUPS = 4
SSM_STATE = 128
SSM_CONV = 4
SSM_BC = SSM_GROUPS * SSM_STATE
SSM_CONV_DIM = SSM_INNER + 2 * SSM_BC
SSM_GROUP_DIM = SSM_INNER // SSM_GROUPS

SC_DIM = D_MODEL
SC_WIDTH = 3

N_HEADS = 16
HEAD_DIM = D_MODEL // N_HEADS
ATT_DIM = N_HEADS * HEAD_DIM
ATT_PAST_CHUNKS = 8
ATT_PAST = ATT_PAST_CHUNKS * CHUNK
ATT_BAND = (ATT_PAST_CHUNKS + 1) * CHUNK
MAX_REL = 128

N_BRANCH = 3
D_FF = 4 * D_MODEL

IN_SPLITS = (SSM_INNER, SSM_CONV_DIM, SSM_HEADS, SC_DIM, SC_DIM, SC_DIM, ATT_DIM, ATT_DIM, ATT_DIM,
             N_BRANCH * D_MODEL)
IN_OFFSETS = tuple(int(sum(IN_SPLITS[:i])) for i in range(len(IN_SPLITS) + 1))

NEG_INF = -1e30

LANES = 128
SUBLANES = 8
MIB = 1024 * 1024

ATT_Q_CHUNKS = 4
ATT_Q_BLOCK = ATT_Q_CHUNKS * CHUNK
ATT_K_BLOCKS = (ATT_PAST_CHUNKS + ATT_Q_CHUNKS) // ATT_Q_CHUNKS
ATT_WINDOW = ATT_K_BLOCKS * ATT_Q_BLOCK
TOEPLITZ_COLS = 1024
SAMPLE_KEY_TILE = 2048
SSD_CHUNKS_PER_STEP = 8
PROMPT_ROW_TILE = 1024


def _cparams(semantics, vmem_mib):
    return pltpu.CompilerParams(dimension_semantics=semantics, vmem_limit_bytes=vmem_mib * MIB)


def _dot(a, b):
    return jnp.dot(a, b, preferred_element_type=F32)


def _dot_nt(a, b):
    return lax.dot_general(a, b, (((1,), (1,)), ((), ())), preferred_element_type=F32)


def _split_bf16(v, parts):
    out = []
    r = v
    for _ in range(parts):
        p = r.astype(BF16)
        out.append(p)
        r = r - p.astype(F32)
    return out


def _dot_exact_lhs(vs, m, parts):
    rows = vs[0].shape[0]
    terms = [p for v, n in zip(vs, parts) for p in _split_bf16(v, n)]
    prod = _dot(jnp.concatenate(terms, axis=0), m)
    outs, at = [], 0
    for n in parts:
        acc = prod[at * rows:(at + 1) * rows]
        for j in range(1, n):
            acc = acc + prod[(at + j) * rows:(at + j + 1) * rows]
        outs.append(acc)
        at += n
    return outs


def _dot_exact_rhs(m, v, parts):
    cols = v.shape[1]
    prod = _dot(m, jnp.concatenate(_split_bf16(v, parts), axis=1))
    acc = prod[:, 0:cols]
    for j in range(1, parts):
        acc = acc + prod[:, j * cols:(j + 1) * cols]
    return acc


def _prompt_tile(i):
    return jnp.maximum(i - 1, 0)


def _two_group_kernel(body, pre, *, n_shared, n_in, n_out, row_axis):
    def kernel(*refs):
        shared = refs[:n_shared]
        in_p = refs[n_shared:n_shared + n_in]
        in_s = refs[n_shared + n_in:n_shared + 2 * n_in]
        o0 = n_shared + 2 * n_in
        out_p = refs[o0:o0 + n_out]
        out_s = refs[o0 + n_out:o0 + 2 * n_out]
        scratch = refs[o0 + 2 * n_out:]
        if pre is not None:
            pre(shared, scratch)
        i = pl.program_id(row_axis)

        @pl.when(i == 0)
        def _():
            body(shared, in_s, out_s, scratch, 1)

        @pl.when(i > 0)
        def _():
            body(shared, in_p, out_p, scratch, 0)

    return kernel


def _two_group_call(body, pre, *, grid, row_axis, shared, ins_p, ins_s, outs_p, outs_s, scratch_shapes, vmem_mib,
                    name):
    assert len(ins_p) == len(ins_s) and len(outs_p) == len(outs_s)
    arrays = [a for a, _ in shared + ins_p + ins_s]
    in_specs = [s for _, s in shared + ins_p + ins_s]
    res = pl.pallas_call(
        _two_group_kernel(body, pre, n_shared=len(shared), n_in=len(ins_p), n_out=len(outs_p), row_axis=row_axis),
        grid=grid,
        in_specs=in_specs,
        out_specs=[s for _, s in outs_p + outs_s],
        out_shape=[a for a, _ in outs_p + outs_s],
        scratch_shapes=scratch_shapes,
        compiler_params=_cparams(("arbitrary",) * len(grid), vmem_mib),
        name=name,
    )(*arrays)
    return res[:len(outs_p)], res[len(outs_p):]


def _row_specs(tm, ts, cols, *, col_of=None, two_d_grid=True):
    if not two_d_grid:
        return (pl.BlockSpec((tm, cols), lambda i, *_: (_prompt_tile(i), 0)),
                pl.BlockSpec((ts, cols), lambda i, *_: (0, 0)))
    col_of = col_of or (lambda j: j)
    return (pl.BlockSpec((tm, cols), lambda j, i: (_prompt_tile(i), col_of(j))),
            pl.BlockSpec((ts, cols), lambda j, i: (0, col_of(j))))


def _rmsnorm_body(shared, ins, outs, scratch, group):
    x = ins[0][...]
    ms = jnp.mean(x * x, axis=-1, keepdims=True)
    outs[0][...] = (x * lax.rsqrt(ms + EPS) * shared[0][...]).astype(outs[0].dtype)


def _rmsnorm(xp, xs, g, tm):
    d = xp.shape[1]
    npt, ts = xp.shape[0] // tm, xs.shape[0]
    sp, ss = _row_specs(tm, ts, d, two_d_grid=False)
    (op,), (os_,) = _two_group_call(
        _rmsnorm_body, None, grid=(npt + 1,), row_axis=0,
        shared=[(g.reshape(1, d), pl.BlockSpec((1, d), lambda i: (0, 0)))],
        ins_p=[(xp, sp)], ins_s=[(xs, ss)],
        outs_p=[(jax.ShapeDtypeStruct(xp.shape, BF16), sp)], outs_s=[(jax.ShapeDtypeStruct(xs.shape, BF16), ss)],
        scratch_shapes=[], vmem_mib=40, name="rmsnorm")
    return op, os_


def _cast_kernel(w_ref, o_ref):
    o_ref[...] = w_ref[0].astype(o_ref.dtype)


def _cast_bf16(w, layer, *, tr):
    _, r, c = w.shape
    return pl.pallas_call(
        _cast_kernel,
        grid=(r // tr,),
        in_specs=[pl.BlockSpec((1, tr, c), lambda i: (layer, i, 0))],
        out_specs=pl.BlockSpec((tr, c), lambda i: (i, 0)),
        out_shape=jax.ShapeDtypeStruct((r, c), BF16),
        compiler_params=_cparams(("arbitrary",), 40),
        name="cast_bf16",
    )(w)


def _wspec(k, tn, layer, row0):
    assert row0 % SUBLANES == 0 and tn % SUBLANES == 0
    return pl.BlockSpec((pl.Element(1), pl.Element(tn), pl.Element(k)),
                        lambda j, i: (layer, pl.multiple_of(row0 + j * tn, SUBLANES), 0))


def _load_weight(w_ref, w16_ref, slot=None, valid_rows=None):
    @pl.when(pl.program_id(1) == 0)
    def _():
        w = w_ref[0]
        if valid_rows is not None:
            rows = lax.broadcasted_iota(jnp.int32, w.shape, 0)
            w = jnp.where(rows < valid_rows, w, 0.0)
        if slot is None:
            w16_ref[...] = w.astype(BF16)
        else:
            w16_ref[slot] = w.astype(BF16)


def _proj_outs(out_kinds, tp, ts, n, tm, tn, npt, tail_rows, tiles_per_seq):
    outs_p, outs_s = [], []
    for kind in out_kinds:
        sp, ss = _row_specs(tm, ts, tn)
        if kind == "tail":
            nseq = tp // (tm * tiles_per_seq)
            outs_p.append((jax.ShapeDtypeStruct((nseq * tail_rows, n), F32),
                           pl.BlockSpec((tail_rows, tn), lambda j, i: (_prompt_tile(i) // tiles_per_seq, j))))
            outs_s.append((jax.ShapeDtypeStruct((ts, n), F32), ss))
        else:
            dt = F32 if kind == "f32" else BF16
            outs_p.append((jax.ShapeDtypeStruct((tp, n), dt), sp))
            outs_s.append((jax.ShapeDtypeStruct((ts, n), dt), ss))
    return outs_p, outs_s


def _store_proj(y, out_kinds, outs, cols=slice(None), scale=1.0):
    for kind, o_ref in zip(out_kinds, outs):
        if kind == "tail":
            o_ref[:, cols] = y[y.shape[0] - o_ref.shape[0]:]
        elif kind == "bf16":
            o_ref[:, cols] = (y * scale).astype(BF16) if scale != 1.0 else y.astype(BF16)
        else:
            o_ref[:, cols] = y


def _proj_act(xnp, xns, wt, layer, row0, n, *, tm, tn, act, out_kinds, bias=None, valid_rows=None, tail_rows=None,
              tiles_per_seq=1, name):
    (tp, k), ts = xnp.shape, xns.shape[0]
    npt = tp // tm
    has_bias = bias is not None

    def pre(shared, scratch):
        _load_weight(shared[0], scratch[0], valid_rows=valid_rows)

    def body(shared, ins, outs, scratch, group):
        acc = _dot_nt(ins[0][...], scratch[0][...])
        if has_bias:
            acc = acc + shared[1][...]
        _store_proj(act(acc), out_kinds, outs)

    shared = [(wt, _wspec(k, tn, layer, row0))]
    if has_bias:
        shared.append((bias.reshape(1, n), pl.BlockSpec((1, tn), lambda j, i: (0, j))))
    xp_spec, xs_spec = _row_specs(tm, ts, k, col_of=lambda j: 0)
    outs_p, outs_s = _proj_outs(out_kinds, tp, ts, n, tm, tn, npt, tail_rows, tiles_per_seq)
    return _two_group_call(body, pre, grid=(n // tn, npt + 1), row_axis=1, shared=shared,
                           ins_p=[(xnp, xp_spec)], ins_s=[(xns, xs_spec)], outs_p=outs_p, outs_s=outs_s,
                           scratch_shapes=[pltpu.VMEM((tn, k), BF16)], vmem_mib=56, name=name)


def _proj_headnorm(xnp, xns, wt, layer, row0, n, g, *, tm, tn, scale, out_kinds, tail_rows=None, tiles_per_seq=1,
                   name):
    (tp, k), ts = xnp.shape, xns.shape[0]
    npt = tp // tm

    def pre(shared, scratch):
        _load_weight(shared[0], scratch[0])

    def body(shared, ins, outs, scratch, group):
        acc = _dot_nt(ins[0][...], scratch[0][...])
        gain = shared[1][...]
        for h in range(tn // HEAD_DIM):
            sl = slice(h * HEAD_DIM, (h + 1) * HEAD_DIM)
            blk = acc[:, sl]
            ms = jnp.mean(blk * blk, axis=-1, keepdims=True)
            _store_proj(blk * lax.rsqrt(ms + EPS) * gain, out_kinds, outs, cols=sl, scale=scale)

    shared = [(wt, _wspec(k, tn, layer, row0)),
              (g.reshape(1, HEAD_DIM), pl.BlockSpec((1, HEAD_DIM), lambda j, i: (0, 0)))]
    xp_spec, xs_spec = _row_specs(tm, ts, k, col_of=lambda j: 0)
    outs_p, outs_s = _proj_outs(out_kinds, tp, ts, n, tm, tn, npt, tail_rows, tiles_per_seq)
    return _two_group_call(body, pre, grid=(n // tn, npt + 1), row_axis=1, shared=shared,
                           ins_p=[(xnp, xp_spec)], ins_s=[(xns, xs_spec)], outs_p=outs_p, outs_s=outs_s,
                           scratch_shapes=[pltpu.VMEM((tn, k), BF16)], vmem_mib=56, name=name)


CONV_SUB = 256


def _causal_conv(u, cs, cw_ref, p_ref, carry_ref, st_ref, *, width, nseg, tiles_per_seq):
    tm, tn = u.shape
    seg_len = tm // nseg
    row8 = lax.broadcasted_iota(jnp.int32, (SUBLANES, tn), 0)
    outs = []
    for s in range(nseg):
        seg = u[s * seg_len:(s + 1) * seg_len]
        prev8 = carry_ref[:, cs] if tiles_per_seq > 1 else p_ref[s, :, cs]
        acc = cw_ref[width - 1:width, cs] * seg
        for k in range(1, width):
            sh = pltpu.roll(seg, k, 0)
            first8 = jnp.where(row8 < k, pltpu.roll(prev8, k, 0), sh[0:SUBLANES])
            shk = jnp.concatenate([first8, sh[SUBLANES:]], axis=0)
            acc = acc + cw_ref[width - 1 - k:width - k, cs] * shk
        outs.append(acc)
        st_ref[s, :, cs] = seg[seg_len - SUBLANES:seg_len]
    if tiles_per_seq > 1:
        carry_ref[:, cs] = u[tm - SUBLANES:tm]
    return outs[0] if nseg == 1 else jnp.concatenate(outs, axis=0)


def _proj_conv(kind, xnp, xns, wt, layer, row0s, n, conv_w8, conv_b, prefix_p, prefix_s, *, tm, tn, seq_p, seq_s,
               out_dtype, name):
    (tp, k), ts = xnp.shape, xns.shape[0]
    npt = tp // tm
    tps_p = seq_p // tm
    nseg_s = ts // seq_s
    assert tps_p >= 1 and seq_p % tm == 0 and ts % seq_s == 0
    width = SSM_CONV if kind == "xbc" else SC_WIDTH
    nw = len(row0s)
    has_b = conv_b is not None

    def pre(shared, scratch):
        for slot in range(nw):
            _load_weight(shared[slot], scratch[1], slot=slot)

    def body(shared, ins, outs, scratch, group):
        x_ref, p_ref = ins
        o_ref, st_ref = outs
        carry_ref, w16_ref = scratch
        cw_ref = shared[nw]
        nseg, tps = (1, tps_p) if group == 0 else (nseg_s, 1)
        if tps > 1:
            @pl.when(lax.rem(_prompt_tile(pl.program_id(1)), tps) == 0)
            def _():
                carry_ref[...] = p_ref[0]
        x = x_ref[...]
        for c in range(tn // CONV_SUB):
            cs = slice(c * CONV_SUB, (c + 1) * CONV_SUB)
            if kind == "xbc":
                u = _dot_nt(x, w16_ref[0, cs, :])
            else:
                u = _dot_nt(x, w16_ref[1, cs, :]) * _dot_nt(x, w16_ref[2, cs, :])
            y = _causal_conv(u, cs, cw_ref, p_ref, carry_ref, st_ref, width=width, nseg=nseg, tiles_per_seq=tps)
            if kind == "xbc":
                y = y + shared[nw + 1][:, cs]
                o_ref[:, cs] = y * jax.nn.sigmoid(y)
            else:
                o_ref[:, cs] = (_dot_nt(x, w16_ref[0, cs, :]) * y).astype(o_ref.dtype)

    shared = [(wt, _wspec(k, tn, layer, r)) for r in row0s]
    shared.append((conv_w8, pl.BlockSpec((SUBLANES, tn), lambda j, i: (0, j))))
    if has_b:
        shared.append((conv_b.reshape(1, n), pl.BlockSpec((1, tn), lambda j, i: (0, j))))
    xp_spec, xs_spec = _row_specs(tm, ts, k, col_of=lambda j: 0)
    yp_spec, ys_spec = _row_specs(tm, ts, tn)
    ins_p = [(xnp, xp_spec),
             (prefix_p, pl.BlockSpec((1, SUBLANES, tn), lambda j, i: (_prompt_tile(i) // tps_p, 0, j)))]
    ins_s = [(xns, xs_spec), (prefix_s, pl.BlockSpec((nseg_s, SUBLANES, tn), lambda j, i: (0, 0, j)))]
    outs_p = [(jax.ShapeDtypeStruct((tp, n), out_dtype), yp_spec),
              (jax.ShapeDtypeStruct((npt, SUBLANES, n), F32),
               pl.BlockSpec((1, SUBLANES, tn), lambda j, i: (_prompt_tile(i), 0, j)))]
    outs_s = [(jax.ShapeDtypeStruct((ts, n), out_dtype), ys_spec),
              (jax.ShapeDtypeStruct((nseg_s, SUBLANES, n), F32),
               pl.BlockSpec((nseg_s, SUBLANES, tn), lambda j, i: (0, 0, j)))]
    (yp, tails_p), (ys, tails_s) = _two_group_call(
        body, pre, grid=(n // tn, npt + 1), row_axis=1, shared=shared, ins_p=ins_p, ins_s=ins_s, outs_p=outs_p,
        outs_s=outs_s, scratch_shapes=[pltpu.VMEM((SUBLANES, tn), F32), pltpu.VMEM((nw, tn, k), BF16)],
        vmem_mib=56, name=name)
    return (yp, tails_p[tps_p - 1::tps_p]), (ys, tails_s)


def _ssd_kernel(x_ref, b_ref, c_ref, dt_ref, zs_ref, alog_ref, dx_ref, ng_ref, ech_ref, els_ref, h0_ref,
                y_ref, hl_ref, ht_ref, *, L, cps, nsteps):
    ci = pl.program_id(1)
    hp = LANES // L
    ntiles = SSM_HEADS // hp
    gw = SSM_GROUP_DIM
    tw = hp * SSM_HEAD_DIM
    log2_l = L.bit_length() - 1

    @pl.when(ci == 0)
    def _():
        ht_ref[...] = h0_ref[0].T

    a = -jnp.exp(alog_ref[...])
    ri = lax.broadcasted_iota(jnp.int32, (L, L), 0)
    cj = lax.broadcasted_iota(jnp.int32, (L, L), 1)
    tri = (ri >= cj).astype(BF16)
    lane_blk = jnp.right_shift(lax.broadcasted_iota(jnp.int32, (1, LANES), 1), log2_l)
    row_l = lax.broadcasted_iota(jnp.int32, (L, LANES), 0)
    lane_s = jnp.bitwise_and(lax.broadcasted_iota(jnp.int32, (L, LANES), 1), L - 1)
    causal = row_l >= lane_s
    rb = jnp.right_shift(lax.broadcasted_iota(jnp.int32, (LANES, tw), 0), log2_l)
    cb_ = jnp.right_shift(lax.broadcasted_iota(jnp.int32, (LANES, tw), 1), SSM_HEAD_DIM.bit_length() - 1)
    blockdiag = rb == cb_
    tile_rows = lambda v: jnp.concatenate([v] * hp, axis=0)
    zpad = jnp.zeros((LANES - L, gw), F32)

    for r in range(cps):
        rows = slice(r * L, (r + 1) * L)
        x = x_ref[rows, :]
        bm = b_ref[rows, :]
        cm = c_ref[rows, :]
        dt = dt_ref[rows, :]
        da = dt * a

        acum = _dot_exact_rhs(tri, da, 3)
        eacum = jnp.exp(acum)
        dend = jnp.exp(acum[L - 1:L, :] - acum)
        w = dt * dend

        ech = ech_ref[...]
        if L == SSM_HEAD_DIM:
            cexp = jnp.concatenate(
                [jnp.broadcast_to(acum[:, h:h + 1], (L, SSM_HEAD_DIM)) for h in range(SSM_HEADS)], axis=1)
        else:
            (cexp,) = _dot_exact_lhs([acum], els_ref[...], [3])
        wx, ex = _dot_exact_lhs([w, eacum], ech, [2, 2])

        acum_t = tile_rows(acum).T
        dt_t = tile_rows(dt).T

        bsq = [tile_rows(bm[:, g * SSM_STATE:(g + 1) * SSM_STATE]) for g in range(SSM_GROUPS)]
        cbt = [_dot_nt(cm[:, g * SSM_STATE:(g + 1) * SSM_STATE].astype(BF16), bsq[g].astype(BF16))
               for g in range(SSM_GROUPS)]

        yd = []
        for t in range(ntiles):
            h_first = t * hp
            g = h_first // (SSM_HEADS // SSM_GROUPS)
            r_row = acum_t[h_first:h_first + 1, :]
            d_row = dt_t[h_first:h_first + 1, :]
            for jj in range(1, hp):
                sel = lane_blk == jj
                r_row = jnp.where(sel, acum_t[h_first + jj:h_first + jj + 1, :], r_row)
                d_row = jnp.where(sel, dt_t[h_first + jj:h_first + jj + 1, :], d_row)
            diff = cexp[:, t * LANES:(t + 1) * LANES] - r_row
            dec = jnp.exp(jnp.where(causal, diff, -jnp.inf))
            sc = (cbt[g] * dec * d_row).astype(BF16)
            xs = tile_rows(x[:, t * tw:(t + 1) * tw])
            rhs = jnp.where(blockdiag, xs, 0.0).astype(BF16)
            yd.append(_dot(sc, rhs))
        y = jnp.concatenate(yd, axis=1)

        xw = x * wx
        for g in range(SSM_GROUPS):
            gs = slice(g * gw, (g + 1) * gw)
            h_in = ht_ref[:, gs]
            y_off = _dot(cm[:, g * SSM_STATE:(g + 1) * SSM_STATE].astype(BF16), h_in.astype(BF16))
            yg = y[:, gs] + y_off * ex[:, gs] + dx_ref[:, gs] * x[:, gs]
            yg = yg * zs_ref[rows, gs]
            ms = jnp.mean(yg * yg, axis=-1, keepdims=True)
            y_ref[rows, gs] = (yg * lax.rsqrt(ms + EPS) * ng_ref[:, gs]).astype(y_ref.dtype)
            bm_t = bsq[g].T.astype(BF16)
            upd = jnp.concatenate([xw[:, gs], zpad], axis=0).astype(BF16)
            ht_ref[:, gs] = h_in * ex[L - 1:L, gs] + _dot(bm_t, upd)

    @pl.when(ci == nsteps - 1)
    def _():
        hl_ref[0] = ht_ref[...].T


def _ssd(xbc_act, dt, zs, a_log128, d_x, norm_g, ech, els, h0, *, L, cps, seq_len):
    t = xbc_act.shape[0]
    nseq = h0.shape[0]
    rows = L * cps
    ns = seq_len // rows
    rmap = lambda b, c: (b * ns + c, 0)
    cmap = lambda b, c: (0, 0)
    nb = SSM_INNER // SSM_BC
    return pl.pallas_call(
        functools.partial(_ssd_kernel, L=L, cps=cps, nsteps=ns),
        grid=(nseq, ns),
        in_specs=[pl.BlockSpec((rows, SSM_INNER), rmap),
                  pl.BlockSpec((rows, SSM_BC), lambda b, c: (b * ns + c, nb)),
                  pl.BlockSpec((rows, SSM_BC), lambda b, c: (b * ns + c, nb + 1)),
                  pl.BlockSpec((rows, LANES), rmap),
                  pl.BlockSpec((rows, SSM_INNER), rmap),
                  pl.BlockSpec((1, LANES), cmap),
                  pl.BlockSpec((1, SSM_INNER), cmap),
                  pl.BlockSpec((1, SSM_INNER), cmap),
                  pl.BlockSpec(ech.shape, cmap),
                  pl.BlockSpec(els.shape, cmap),
                  pl.BlockSpec((1, SSM_INNER, SSM_STATE), lambda b, c: (b, 0, 0))],
        out_specs=[pl.BlockSpec((rows, SSM_INNER), rmap),
                   pl.BlockSpec((1, SSM_INNER, SSM_STATE), lambda b, c: (b, 0, 0))],
        out_shape=[jax.ShapeDtypeStruct((t, SSM_INNER), BF16),
                   jax.ShapeDtypeStruct((nseq, SSM_INNER, SSM_STATE), F32)],
        scratch_shapes=[pltpu.VMEM((SSM_STATE, SSM_INNER), F32)],
        compiler_params=_cparams(("arbitrary", "arbitrary"), 48),
        name=f"ssd_L{L}",
    )(xbc_act, xbc_act, xbc_act, dt, zs, a_log128, d_x, norm_g, ech, els, h0)


def _softmax_pv(s_parts, v_parts):
    m = functools.reduce(jnp.maximum, [jnp.max(s, axis=1, keepdims=True) for s in s_parts])
    l = None
    o = None
    for s, v in zip(s_parts, v_parts):
        p = jnp.exp(s - m)
        ls = jnp.sum(p, axis=1, keepdims=True)
        os_ = _dot(p.astype(BF16), v)
        l = ls if l is None else l + ls
        o = os_ if o is None else o + os_
    return o / l


def _attn_prompt_kernel(q_ref, k0_ref, k1_ref, k2_ref, v0_ref, v1_ref, v2_ref, bias_ref, o_ref):
    i = pl.program_id(1)
    k_refs = (k0_ref, k1_ref, k2_ref)
    v_refs = (v0_ref, v1_ref, v2_ref)

    first_valid = (ATT_K_BLOCKS - 1 - i) * ATT_Q_BLOCK
    kidx = lax.broadcasted_iota(jnp.int32, (ATT_Q_BLOCK, ATT_Q_BLOCK), 1)
    for h in range(N_HEADS):
        sl = slice(h * HEAD_DIM, (h + 1) * HEAD_DIM)
        q = q_ref[:, sl]
        s_parts = []
        for kb in range(ATT_K_BLOCKS):
            s = _dot_nt(q, k_refs[kb][:, sl]) + bias_ref[h, :, kb * ATT_Q_BLOCK:(kb + 1) * ATT_Q_BLOCK]
            s_parts.append(jnp.where(kidx + kb * ATT_Q_BLOCK >= first_valid, s, NEG_INF))
        o = _softmax_pv(s_parts, [v_refs[kb][:, sl] for kb in range(ATT_K_BLOCKS)])
        o_ref[:, sl] = o.astype(o_ref.dtype)


def _attn_prompt(q, k, v, bias, *, nseq, seq_len):
    t = q.shape[0]
    nqb = seq_len // ATT_Q_BLOCK
    qmap = lambda b, i: (b * nqb + i, 0)

    def kmap(back):
        return lambda b, i: (b * nqb + jnp.maximum(i - back, 0), 0)

    blk = (ATT_Q_BLOCK, ATT_DIM)
    kv_specs = [pl.BlockSpec(blk, kmap(ATT_K_BLOCKS - 1 - kb)) for kb in range(ATT_K_BLOCKS)]
    return pl.pallas_call(
        _attn_prompt_kernel,
        grid=(nseq, nqb),
        in_specs=[pl.BlockSpec(blk, qmap)] + kv_specs + kv_specs
        + [pl.BlockSpec(bias.shape, lambda b, i: (0, 0, 0))],
        out_specs=pl.BlockSpec(blk, qmap),
        out_shape=jax.ShapeDtypeStruct((t, ATT_DIM), BF16),
        compiler_params=_cparams(("arbitrary", "arbitrary"), 56),
        name="attn_prompt",
    )(q, k, k, k, v, v, v, bias)


def _heads_to_rows(ref):
    return jnp.concatenate([ref[:, h * HEAD_DIM:(h + 1) * HEAD_DIM] for h in range(N_HEADS)], axis=0)


def _attn_sample_kernel(q_ref, kn_ref, vn_ref, kc_ref, vc_ref, bf_ref, bc_ref, px_ref, bn_ref, o_ref):
    t = q_ref.shape[0]
    q2 = _heads_to_rows(q_ref)
    kn2, vn2 = _heads_to_rows(kn_ref), _heads_to_rows(vn_ref)
    npos = SAMPLE_KEY_TILE // N_HEADS
    ntiles = kc_ref.shape[2] // npos
    nfar = ntiles - bc_ref.shape[1] // npos
    assert t & (t - 1) == 0
    row_head = jnp.right_shift(lax.broadcasted_iota(jnp.int32, (N_HEADS * t, SAMPLE_KEY_TILE), 0),
                               t.bit_length() - 1)
    col_head = jnp.bitwise_and(lax.broadcasted_iota(jnp.int32, (N_HEADS * t, SAMPLE_KEY_TILE), 1), N_HEADS - 1)
    same_head = row_head == col_head
    far_bias = jnp.where(same_head, bf_ref[...], NEG_INF)
    s_parts, v_parts = [], []
    for c in range(ntiles):
        ps = slice(c * npos, (c + 1) * npos)
        kc = kc_ref[0, 0, ps, :, :].reshape(SAMPLE_KEY_TILE, HEAD_DIM)
        vc = vc_ref[0, 0, ps, :, :].reshape(SAMPLE_KEY_TILE, HEAD_DIM)
        if c < nfar:
            bias = far_bias
        else:
            (near,) = _dot_exact_lhs([bc_ref[:, (c - nfar) * npos:(c - nfar + 1) * npos]], px_ref[...], [3])
            bias = jnp.where(same_head, near, NEG_INF)
        s_parts.append(_dot_nt(q2, kc.astype(BF16)) + bias)
        v_parts.append(vc.astype(BF16))
    s_parts.append(_dot_nt(q2, kn2) + bn_ref[...])
    v_parts.append(vn2)
    o2 = _softmax_pv(s_parts, v_parts)
    for h in range(N_HEADS):
        o_ref[:, h * HEAD_DIM:(h + 1) * HEAD_DIM] = o2[h * t:(h + 1) * t].astype(o_ref.dtype)


def _attn_sample(q, k, v, k_cache, v_cache, layer, bias_far, bias_c, pos_expand, bias_n, *, nseq, seq_len):
    t = q.shape[0]
    blk = (seq_len, ATT_DIM)
    rmap = lambda b: (b, 0)
    const = lambda b: (0, 0)
    cspec = pl.BlockSpec((1, 1) + k_cache.shape[2:], lambda b: (layer, b, 0, 0, 0))
    return pl.pallas_call(
        _attn_sample_kernel,
        grid=(nseq,),
        in_specs=[pl.BlockSpec(blk, rmap), pl.BlockSpec(blk, rmap), pl.BlockSpec(blk, rmap), cspec, cspec,
                  pl.BlockSpec(bias_far.shape, const), pl.BlockSpec(bias_c.shape, const),
                  pl.BlockSpec(pos_expand.shape, const), pl.BlockSpec(bias_n.shape, const)],
        out_specs=pl.BlockSpec(blk, rmap),
        out_shape=jax.ShapeDtypeStruct((t, ATT_DIM), BF16),
        compiler_params=_cparams(("arbitrary",), 56),
        name="attn_sample",
    )(q, k, v, k_cache, v_cache, bias_far, bias_c, pos_expand, bias_n)


def _merge(ys_p, ys_s, w_ssm, w_sc, w_att, layer, gates_p, gates_s, *, tm, tn):
    (tp, k), ts = ys_p[0].shape, ys_s[0].shape[0]
    n = w_ssm.shape[2]
    ncol, npt = n // tn, tp // tm

    def pre(shared, scratch):
        @pl.when(pl.program_id(1) == 0)
        def _():
            for b in range(N_BRANCH):
                scratch[0][b] = shared[b][0].astype(BF16)

    def body(shared, ins, outs, scratch, group):
        w16_ref = scratch[0]
        m = ins[3][...] * _dot(ins[0][...], w16_ref[0])
        m = m + ins[4][...] * _dot(ins[1][...], w16_ref[1])
        m = m + ins[5][...] * _dot(ins[2][...], w16_ref[2])
        outs[0][...] = m.astype(outs[0].dtype)

    rhs = pl.BlockSpec((1, k, tn), lambda j, i: (layer, 0, j))
    lp, ls = _row_specs(tm, ts, k, col_of=lambda j: 0)
    ins_p = [(y, lp) for y in ys_p]
    ins_s = [(y, ls) for y in ys_s]
    for b in range(N_BRANCH):
        gp, gs = _row_specs(tm, ts, tn, col_of=lambda j, b=b: b * ncol + j)
        ins_p.append((gates_p, gp))
        ins_s.append((gates_s, gs))
    op, os_ = _row_specs(tm, ts, tn)
    (mp,), (ms,) = _two_group_call(
        body, pre, grid=(ncol, npt + 1), row_axis=1, shared=[(w_ssm, rhs), (w_sc, rhs), (w_att, rhs)],
        ins_p=ins_p, ins_s=ins_s, outs_p=[(jax.ShapeDtypeStruct((tp, n), BF16), op)],
        outs_s=[(jax.ShapeDtypeStruct((ts, n), BF16), os_)],
        scratch_shapes=[pltpu.VMEM((N_BRANCH, k, tn), BF16)], vmem_mib=56, name="merge")
    return mp, ms


def _wo_body(shared, ins, outs, scratch, group):
    h = ins[0][...] + _dot(ins[1][...], shared[0][...])
    outs[0][...] = h
    ms = jnp.mean(h * h, axis=-1, keepdims=True)
    outs[1][...] = (h * lax.rsqrt(ms + EPS) * shared[1][...]).astype(outs[1].dtype)


def _wo(xp, xs, mp, ms, w_o, g, *, tm):
    (tp, d), ts = xp.shape, xs.shape[0]
    npt = tp // tm
    sp, ss = _row_specs(tm, ts, d, two_d_grid=False)
    const = lambda i: (0, 0)
    (hp, hnp), (hs, hns) = _two_group_call(
        _wo_body, None, grid=(npt + 1,), row_axis=0,
        shared=[(w_o, pl.BlockSpec((d, d), const)), (g.reshape(1, d), pl.BlockSpec((1, d), const))],
        ins_p=[(xp, sp), (mp, sp)], ins_s=[(xs, ss), (ms, ss)],
        outs_p=[(jax.ShapeDtypeStruct((tp, d), F32), sp), (jax.ShapeDtypeStruct((tp, d), BF16), sp)],
        outs_s=[(jax.ShapeDtypeStruct((ts, d), F32), ss), (jax.ShapeDtypeStruct((ts, d), BF16), ss)],
        scratch_shapes=[], vmem_mib=52, name="wo")
    return (hp, hnp), (hs, hns)


def _ffn_body(shared, ins, outs, scratch, group):
    c = pl.program_id(1)
    a = _dot(ins[1][...], shared[0][...])
    a = jnp.square(jnp.maximum(a, 0.0)).astype(BF16)
    contrib = _dot(a, shared[1][...])

    @pl.when(c == 0)
    def _():
        outs[0][...] = ins[0][...] + contrib

    @pl.when(c > 0)
    def _():
        outs[0][...] += contrib

    if len(outs) > 1:
        @pl.when(c == pl.num_programs(1) - 1)
        def _():
            y = outs[0][...]
            ms = jnp.mean(y * y, axis=-1, keepdims=True)
            outs[1][...] = (y * lax.rsqrt(ms + EPS) * shared[2][...]).astype(outs[1].dtype)


def _ffn(hp, hnp, hs, hns, w1, w2, *, tm, tc, next_norm_g=None):
    (tp, d), ts = hp.shape, hs.shape[0]
    dff = w1.shape[1]
    npt = tp // tm
    sp, ss = _row_specs(tm, ts, d, two_d_grid=False)
    shared = [(w1, pl.BlockSpec((d, tc), lambda i, c: (0, c))), (w2, pl.BlockSpec((tc, d), lambda i, c: (c, 0)))]
    outs_p = [(jax.ShapeDtypeStruct((tp, d), F32), sp)]
    outs_s = [(jax.ShapeDtypeStruct((ts, d), F32), ss)]
    if next_norm_g is not None:
        shared.append((next_norm_g.reshape(1, d), pl.BlockSpec((1, d), lambda i, c: (0, 0))))
        outs_p.append((jax.ShapeDtypeStruct((tp, d), BF16), sp))
        outs_s.append((jax.ShapeDtypeStruct((ts, d), BF16), ss))
    return _two_group_call(
        _ffn_body, None, grid=(npt + 1, dff // tc), row_axis=0, shared=shared,
        ins_p=[(hp, sp), (hnp, sp)], ins_s=[(hs, ss), (hns, ss)], outs_p=outs_p, outs_s=outs_s,
        scratch_shapes=[], vmem_mib=52, name="ffn")


def _head_expand(lanes_per_head):
    rows = lax.broadcasted_iota(jnp.int32, (LANES, SSM_HEADS * lanes_per_head), 0)
    cols = lax.broadcasted_iota(jnp.int32, (LANES, SSM_HEADS * lanes_per_head), 1) // lanes_per_head
    return (rows == cols).astype(BF16)


def _bias_tables_kernel(row_ref, toe_ref, pb_ref):
    x = jnp.broadcast_to(row_ref[0], (ATT_Q_BLOCK, TOEPLITZ_COLS))
    t = pltpu.roll(x, 0, 1, stride=1, stride_axis=0)
    toe_ref[0] = t[:toe_ref.shape[1]]
    qi = lax.broadcasted_iota(jnp.int32, (ATT_Q_BLOCK, ATT_WINDOW), 0)
    kj = lax.broadcasted_iota(jnp.int32, (ATT_Q_BLOCK, ATT_WINDOW), 1)
    band = kj - jnp.bitwise_and(qi, -CHUNK)
    pb_ref[0] = jnp.where((band >= 0) & (band < ATT_BAND), t[:, :ATT_WINDOW], NEG_INF)


def _bias_tables(rel_bias, sample_rows):
    tbl = rel_bias.astype(F32)
    far = jnp.broadcast_to(tbl[:, 2 * MAX_REL:], (N_HEADS, ATT_PAST - MAX_REL))
    wrap = jnp.broadcast_to(tbl[:, 2 * MAX_REL:], (N_HEADS, ATT_Q_BLOCK - 1))
    n_near = TOEPLITZ_COLS - far.shape[1] - tbl.shape[1] - wrap.shape[1]
    assert ATT_WINDOW <= far.shape[1] + tbl.shape[1] + n_near
    near = jnp.broadcast_to(tbl[:, :1], (N_HEADS, n_near))
    row = jnp.concatenate([far, tbl[:, ::-1], near, wrap], axis=1).reshape(N_HEADS, 1, TOEPLITZ_COLS)
    return pl.pallas_call(
        _bias_tables_kernel,
        grid=(N_HEADS,),
        in_specs=[pl.BlockSpec((1, 1, TOEPLITZ_COLS), lambda h: (h, 0, 0))],
        out_specs=[pl.BlockSpec((1, sample_rows, TOEPLITZ_COLS), lambda h: (h, 0, 0)),
                   pl.BlockSpec((1, ATT_Q_BLOCK, ATT_WINDOW), lambda h: (h, 0, 0))],
        out_shape=[jax.ShapeDtypeStruct((N_HEADS, sample_rows, TOEPLITZ_COLS), F32),
                   jax.ShapeDtypeStruct((N_HEADS, ATT_Q_BLOCK, ATT_WINDOW), F32)],
        compiler_params=_cparams(("arbitrary",), 32),
        name="bias_tables",
    )(row)


def _sample_bias(toeplitz, t, lc):
    npos = SAMPLE_KEY_TILE // N_HEADS
    far = (lc - MAX_REL) // npos * npos
    same = jnp.arange(N_HEADS)[:, None] == jnp.arange(N_HEADS)[None, :]
    bias_far = jnp.broadcast_to(toeplitz[:, :1, :1], (N_HEADS, t, 1))
    bias_c = toeplitz[:, :t, far:lc].reshape(N_HEADS * t, lc - far)
    pos_expand = (jnp.arange(npos)[:, None] == jnp.arange(SAMPLE_KEY_TILE)[None, :] // N_HEADS).astype(BF16)
    tn_ = toeplitz[:, :t, lc:lc + t]
    bias_n = jnp.where(same[:, None, :, None], tn_[:, :, None, :], NEG_INF)
    return (bias_far.reshape(N_HEADS * t, 1), bias_c, pos_expand, bias_n.reshape(N_HEADS * t, N_HEADS * t))


def _pad_rows_to8(a, axis):
    pad = [(0, 0)] * a.ndim
    pad[axis] = (SUBLANES - a.shape[axis], 0)
    return jnp.pad(a, pad)


def _layer(xp, xs, xn, lw, layer, gp, gs):
    tm, tn = PROMPT_ROW_TILE, 1024
    wt = lw["w_in_t"]
    off = IN_OFFSETS
    seq_p, seq_s = gp["seq_len"], gs["seq_len"]
    tiles_per_seq = seq_p // tm
    tail_rows = min(tm, ATT_PAST, seq_p)

    xnp, xns = xn if xn is not None else _rmsnorm(xp, xs, lw["norm_mix_g"], tm)
    proj = functools.partial(_proj_act, xnp, xns, wt, layer, tm=tm)
    (zp,), (zs,) = proj(off[0], SSM_INNER, tn=tn, act=lambda a: a * jax.nn.sigmoid(a), out_kinds=["f32"],
                        name="proj_z")
    (dtp,), (dts,) = proj(off[2], LANES, tn=LANES, act=jax.nn.softplus, out_kinds=["f32"], bias=lw["dt_bias128"],
                          valid_rows=SSM_HEADS, name="proj_dt")
    (gtp,), (gts,) = proj(off[9], N_BRANCH * D_MODEL, tn=tn, act=jax.nn.sigmoid, out_kinds=["f32"],
                          name="proj_gates")
    (v16p, vtp), (v16s, vts) = proj(off[8], ATT_DIM, tn=tn, act=lambda a: a, out_kinds=["bf16", "tail"],
                                    tail_rows=tail_rows, tiles_per_seq=tiles_per_seq, name="proj_v")
    (q16p,), (q16s,) = _proj_headnorm(xnp, xns, wt, layer, off[6], ATT_DIM, lw["q_norm_g"], tm=tm, tn=tn,
                                      scale=HEAD_DIM ** -0.5, out_kinds=["bf16"], name="proj_q")
    (k16p, ktp), (k16s, kts) = _proj_headnorm(xnp, xns, wt, layer, off[7], ATT_DIM, lw["k_norm_g"], tm=tm, tn=tn,
                                              scale=1.0, out_kinds=["bf16", "tail"], tail_rows=tail_rows,
                                              tiles_per_seq=tiles_per_seq, name="proj_k")
    (xbcp, convp), (xbcs, convs) = _proj_conv(
        "xbc", xnp, xns, wt, layer, [off[1]], SSM_CONV_DIM, lw["ssm_conv_w8"], lw["ssm_conv_b"],
        gp["ssm_conv_prefix"], gs["ssm_conv_prefix"], tm=tm, tn=512, seq_p=seq_p, seq_s=seq_s, out_dtype=F32,
        name="proj_xbc")
    (yscp, scp), (yscs, scs) = _proj_conv(
        "sc", xnp, xns, wt, layer, [off[3], off[4], off[5]], SC_DIM, lw["sc_conv_w8"], None,
        gp["sc_prefix"], gs["sc_prefix"], tm=tm, tn=512, seq_p=seq_p, seq_s=seq_s, out_dtype=BF16, name="proj_sc")

    ssd = functools.partial(_ssd, a_log128=lw["a_log128"], d_x=lw["d_x"], norm_g=lw["ssm_norm_g"], ech=lw["ech"])
    yssmp, hlp = ssd(xbcp, dtp, zp, els=lw["ech"], h0=gp["ssm_h0"], L=CHUNK,
                     cps=min(SSD_CHUNKS_PER_STEP, seq_p // CHUNK), seq_len=seq_p)
    yssms, hls = ssd(xbcs, dts, zs, els=_head_expand(seq_s), h0=gs["ssm_h0"], L=seq_s, cps=1, seq_len=seq_s)

    op = _attn_prompt(q16p, k16p, v16p, lw["prompt_bias"], nseq=gp["nseq"], seq_len=seq_p)
    os_ = _attn_sample(q16s, k16s, v16s, gs["kv_cache"][0], gs["kv_cache"][1], layer, *lw["sample_bias"],
                       nseq=gs["nseq"], seq_len=seq_s)

    mtm = 512
    mp, ms = _merge((yssmp, yscp, op), (yssms, yscs, os_), lw["ssm_out_w"], lw["sc_out_w"], lw["attn_out_w"],
                    layer, gtp, gts, tm=mtm, tn=512)
    (hp, hnp), (hs, hns) = _wo(xp, xs, mp, ms, lw["w_o"], lw["norm_ffn_g"], tm=mtm)
    fp, fs = _ffn(hp, hnp, hs, hns, lw["ffn_w1"], lw["ffn_w2"], tm=mtm, tc=1024, next_norm_g=lw.get("next_norm_g"))
    xn_next = (fp[1], fs[1]) if len(fp) > 1 else None
    return (fp[0], (ktp, vtp, hlp, convp, scp)), (fs[0], (kts, vts, hls, convs, scs)), xn_next


def kernel(x_prompt, x_sample, cache_attn_k, cache_attn_v, state_ssm, state_ssm_conv, state_short_conv,
           norm_mix_g, w_in, ssm_conv_w, ssm_conv_b, ssm_dt_bias, ssm_a_log, ssm_d, ssm_norm_g, ssm_out_w,
           sc_conv_w, sc_out_w, q_norm_g, k_norm_g, rel_bias, attn_out_w, w_o, norm_ffn_g, ffn_w1, ffn_w2):
    bp, lp, d = x_prompt.shape
    bs, ls, _ = x_sample.shape
    lc = cache_attn_k.shape[2]
    assert lc == ATT_PAST and lc + ls <= TOEPLITZ_COLS and ls <= ATT_Q_BLOCK
    ech = _head_expand(SSM_HEAD_DIM)
    w_in_t = jnp.swapaxes(w_in, 1, 2)

    yp = x_prompt.reshape(bp * lp, d)
    ys = x_sample.reshape(bs * ls, d)
    new_p, new_s = [], []
    xn = None
    for l in range(DEPTH):
        toeplitz, prompt_bias = _bias_tables(rel_bias[l], ls)
        lw = {
            "norm_mix_g": norm_mix_g[l], "norm_ffn_g": norm_ffn_g[l],
            "w_in_t": w_in_t,
            "dt_bias128": jnp.pad(ssm_dt_bias[l].astype(F32), (0, LANES - SSM_HEADS)),
            "a_log128": jnp.pad(ssm_a_log[l].astype(F32), (0, LANES - SSM_HEADS)).reshape(1, LANES),
            "d_x": jnp.repeat(ssm_d[l].astype(F32), SSM_HEAD_DIM).reshape(1, SSM_INNER),
            "ssm_norm_g": ssm_norm_g[l].astype(F32).reshape(1, SSM_INNER),
            "ssm_conv_w8": jnp.pad(ssm_conv_w[l].astype(F32), ((0, SUBLANES - SSM_CONV), (0, 0))),
            "ssm_conv_b": ssm_conv_b[l].astype(F32),
            "sc_conv_w8": jnp.pad(sc_conv_w[l].astype(F32), ((0, SUBLANES - SC_WIDTH), (0, 0))),
            "q_norm_g": q_norm_g[l].astype(F32), "k_norm_g": k_norm_g[l].astype(F32),
            "ssm_out_w": ssm_out_w, "sc_out_w": sc_out_w, "attn_out_w": attn_out_w,
            "w_o": _cast_bf16(w_o, l, tr=1024),
            "ffn_w1": _cast_bf16(ffn_w1, l, tr=256), "ffn_w2": _cast_bf16(ffn_w2, l, tr=1024),
            "ech": ech,
            "prompt_bias": prompt_bias,
            "sample_bias": _sample_bias(toeplitz, ls, lc),
        }
        gp = dict(nseq=bp, seq_len=lp,
                  ssm_conv_prefix=jnp.zeros((bp, SUBLANES, SSM_CONV_DIM), F32),
                  ssm_h0=jnp.zeros((bp, SSM_INNER, SSM_STATE), F32),
                  sc_prefix=jnp.zeros((bp, SUBLANES, SC_DIM), F32))
        gs = dict(nseq=bs, seq_len=ls,
                  ssm_conv_prefix=_pad_rows_to8(state_ssm_conv[l].astype(F32), 1),
                  ssm_h0=state_ssm[l].astype(F32).reshape(bs, SSM_INNER, SSM_STATE),
                  sc_prefix=_pad_rows_to8(state_short_conv[l].astype(F32), 1),
                  kv_cache=(cache_attn_k, cache_attn_v))
        if l + 1 < DEPTH:
            lw["next_norm_g"] = norm_mix_g[l + 1]
        (yp, st_p), (ys, st_s), xn = _layer(yp, ys, xn, lw, l, gp, gs)
        new_p.append(st_p)
        new_s.append(st_s)

    keep = min(ATT_PAST, lp)

    def stack(states, fn):
        return jnp.stack([fn(s) for s in states])

    hshape = lambda b: (b, SSM_HEADS, SSM_HEAD_DIM, SSM_STATE)
    return (
        yp.reshape(bp, lp, d),
        ys.reshape(bs, ls, d),
        stack(new_p, lambda s: s[0].reshape(bp, keep, N_HEADS, HEAD_DIM)),
        stack(new_p, lambda s: s[1].reshape(bp, keep, N_HEADS, HEAD_DIM)),
        stack(new_s, lambda s: s[0].reshape(bs, ls, N_HEADS, HEAD_DIM)),
        stack(new_s, lambda s: s[1].reshape(bs, ls, N_HEADS, HEAD_DIM)),
        stack(new_p, lambda s: s[2].reshape(hshape(bp))),
        stack(new_s, lambda s: s[2].reshape(hshape(bs))),
        stack(new_p, lambda s: s[3][:, SUBLANES - (SSM_CONV - 1):]),
        stack(new_s, lambda s: s[3][:, SUBLANES - (SSM_CONV - 1):]),
        stack(new_p, lambda s: s[4][:, SUBLANES - (SC_WIDTH - 1):]),
        stack(new_s, lambda s: s[4][:, SUBLANES - (SC_WIDTH - 1):]),
    )
```

```python
import functools
import math

import jax
import jax.numpy as jnp
from jax import lax
from jax.experimental import pallas as pl
from jax.experimental.pallas import tpu as pltpu

F32 = jnp.float32
BF16 = jnp.bfloat16

D_MODEL = 2048
DEPTH = 2
CHUNK = 64
EPS = 1e-6

SSM_INNER = D_MODEL
SSM_HEAD_DIM = 64
SSM_HEADS = SSM_INNER // SSM_HEAD_DIM
SSM_GROUPS = 4
SSM_STATE = 128
SSM_CONV = 4
SSM_BC = SSM_GROUPS * SSM_STATE
SSM_CONV_DIM = SSM_INNER + 2 * SSM_BC
SSM_GROUP_DIM = SSM_INNER // SSM_GROUPS

SC_DIM = D_MODEL
SC_WIDTH = 3

N_HEADS = 16
HEAD_DIM = D_MODEL // N_HEADS
ATT_DIM = N_HEADS * HEAD_DIM
ATT_PAST_CHUNKS = 8
ATT_PAST = ATT_PAST_CHUNKS * CHUNK
ATT_BAND = (ATT_PAST_CHUNKS + 1) * CHUNK
MAX_REL = 128

N_BRANCH = 3
D_FF = 4 * D_MODEL

IN_SPLITS = (SSM_INNER, SSM_CONV_DIM, SSM_HEADS, SC_DIM, SC_DIM, SC_DIM, ATT_DIM, ATT_DIM, ATT_DIM,
             N_BRANCH * D_MODEL)
IN_OFFSETS = tuple(int(sum(IN_SPLITS[:i])) for i in range(len(IN_SPLITS) + 1))

NEG_INF = -1e30

LANES = 128
SUBLANES = 8
MIB = 1024 * 1024

ATT_Q_CHUNKS = 4
ATT_Q_BLOCK = ATT_Q_CHUNKS * CHUNK
ATT_K_BLOCKS = (ATT_PAST_CHUNKS + ATT_Q_CHUNKS) // ATT_Q_CHUNKS
ATT_WINDOW = ATT_K_BLOCKS * ATT_Q_BLOCK
TOEPLITZ_COLS = 1024
SAMPLE_KEY_TILE = 2048
SSD_CHUNKS_PER_STEP = 8
SIDE_CAST_CHUNKS = 32
PROMPT_ROW_TILE = 1024


def _cparams(semantics, vmem_mib):
    return pltpu.CompilerParams(dimension_semantics=semantics, vmem_limit_bytes=vmem_mib * MIB)


def _dot(a, b):
    return jnp.dot(a, b, preferred_element_type=F32)


def _dot_nt(a, b):
    return lax.dot_general(a, b, (((1,), (1,)), ((), ())), preferred_element_type=F32)


def _split_bf16(v, parts):
    out = []
    r = v
    for _ in range(parts):
        p = r.astype(BF16)
        out.append(p)
        r = r - p.astype(F32)
    return out


def _dot_exact_lhs(vs, m, parts):
    rows = vs[0].shape[0]
    terms = [p for v, n in zip(vs, parts) for p in _split_bf16(v, n)]
    prod = _dot(jnp.concatenate(terms, axis=0), m)
    outs, at = [], 0
    for n in parts:
        acc = prod[at * rows:(at + 1) * rows]
        for j in range(1, n):
            acc = acc + prod[(at + j) * rows:(at + j + 1) * rows]
        outs.append(acc)
        at += n
    return outs


def _dot_exact_rhs(m, v, parts):
    cols = v.shape[1]
    prod = _dot(m, jnp.concatenate(_split_bf16(v, parts), axis=1))
    acc = prod[:, 0:cols]
    for j in range(1, parts):
        acc = acc + prod[:, j * cols:(j + 1) * cols]
    return acc


def _prompt_tile(i):
    return jnp.maximum(i - 1, 0)


def _two_group_kernel(body, pre, side, *, n_shared, n_in, n_out, n_side, row_axis):
    def kernel(*refs):
        shared = refs[:n_shared]
        in_p = refs[n_shared:n_shared + n_in]
        in_s = refs[n_shared + n_in:n_shared + 2 * n_in]
        o0 = n_shared + 2 * n_in
        out_p = refs[o0:o0 + n_out]
        out_s = refs[o0 + n_out:o0 + 2 * n_out]
        side_outs = refs[o0 + 2 * n_out:o0 + 2 * n_out + n_side]
        scratch = refs[o0 + 2 * n_out + n_side:]
        if pre is not None:
            pre(shared, scratch)
        if side is not None:
            side(shared, side_outs)
        i = pl.program_id(row_axis)

        @pl.when(i == 0)
        def _():
            body(shared, in_s, out_s, scratch, 1)

        @pl.when(i > 0)
        def _():
            body(shared, in_p, out_p, scratch, 0)

    return kernel


def _two_group_call(body, pre, *, grid, row_axis, shared, ins_p, ins_s, outs_p, outs_s, scratch_shapes, vmem_mib,
                    name, side=None, side_outs=()):
    assert len(ins_p) == len(ins_s) and len(outs_p) == len(outs_s)
    side_outs = list(side_outs)
    arrays = [a for a, _ in shared + ins_p + ins_s]
    in_specs = [s for _, s in shared + ins_p + ins_s]
    res = pl.pallas_call(
        _two_group_kernel(body, pre, side, n_shared=len(shared), n_in=len(ins_p), n_out=len(outs_p),
                          n_side=len(side_outs), row_axis=row_axis),
        grid=grid,
        in_specs=in_specs,
        out_specs=[s for _, s in outs_p + outs_s + side_outs],
        out_shape=[a for a, _ in outs_p + outs_s + side_outs],
        scratch_shapes=scratch_shapes,
        compiler_params=_cparams(("arbitrary",) * len(grid), vmem_mib),
        name=name,
    )(*arrays)
    n = len(outs_p)
    if side is None:
        return res[:n], res[n:]
    return res[:n], res[n:2 * n], res[2 * n:]


def _row_specs(tm, ts, cols, *, col_of=None, two_d_grid=True):
    if not two_d_grid:
        return (pl.BlockSpec((tm, cols), lambda i, *_: (_prompt_tile(i), 0)),
                pl.BlockSpec((ts, cols), lambda i, *_: (0, 0)))
    col_of = col_of or (lambda j: j)
    return (pl.BlockSpec((tm, cols), lambda j, i: (_prompt_tile(i), col_of(j))),
            pl.BlockSpec((ts, cols), lambda j, i: (0, col_of(j))))


def _rmsnorm_body(shared, ins, outs, scratch, group):
    x = ins[0][...]
    ms = jnp.mean(x * x, axis=-1, keepdims=True)
    outs[0][...] = (x * lax.rsqrt(ms + EPS) * shared[0][...]).astype(outs[0].dtype)


def _rmsnorm(xp, xs, g, tm):
    d = xp.shape[1]
    npt, ts = xp.shape[0] // tm, xs.shape[0]
    sp, ss = _row_specs(tm, ts, d, two_d_grid=False)
    (op,), (os_,) = _two_group_call(
        _rmsnorm_body, None, grid=(npt + 1,), row_axis=0,
        shared=[(g.reshape(1, d), pl.BlockSpec((1, d), lambda i: (0, 0)))],
        ins_p=[(xp, sp)], ins_s=[(xs, ss)],
        outs_p=[(jax.ShapeDtypeStruct(xp.shape, BF16), sp)], outs_s=[(jax.ShapeDtypeStruct(xs.shape, BF16), ss)],
        scratch_shapes=[], vmem_mib=40, name="rmsnorm")
    return op, os_


def _cast_kernel(w_ref, o_ref):
    o_ref[...] = w_ref[0].astype(o_ref.dtype)


def _cast_bf16(w, layer, *, tr):
    _, r, c = w.shape
    return pl.pallas_call(
        _cast_kernel,
        grid=(r // tr,),
        in_specs=[pl.BlockSpec((1, tr, c), lambda i: (layer, i, 0))],
        out_specs=pl.BlockSpec((tr, c), lambda i: (i, 0)),
        out_shape=jax.ShapeDtypeStruct((r, c), BF16),
        compiler_params=_cparams(("arbitrary",), 40),
        name="cast_bf16",
    )(w)


def _wspec(k, tn, layer, row0):
    assert row0 % SUBLANES == 0 and tn % SUBLANES == 0
    return pl.BlockSpec((pl.Element(1), pl.Element(tn), pl.Element(k)),
                        lambda j, i: (layer, pl.multiple_of(row0 + j * tn, SUBLANES), 0))


def _load_weight(w_ref, w16_ref, slot=None, valid_rows=None):
    @pl.when(pl.program_id(1) == 0)
    def _():
        w = w_ref[0]
        if valid_rows is not None:
            rows = lax.broadcasted_iota(jnp.int32, w.shape, 0)
            w = jnp.where(rows < valid_rows, w, 0.0)
        if slot is None:
            w16_ref[...] = w.astype(BF16)
        else:
            w16_ref[slot] = w.astype(BF16)


def _proj_outs(out_kinds, tp, ts, n, tm, tn, npt, tail_rows, tiles_per_seq):
    outs_p, outs_s = [], []
    for kind in out_kinds:
        sp, ss = _row_specs(tm, ts, tn)
        if kind == "tail":
            nseq = tp // (tm * tiles_per_seq)
            outs_p.append((jax.ShapeDtypeStruct((nseq * tail_rows, n), F32),
                           pl.BlockSpec((tail_rows, tn), lambda j, i: (_prompt_tile(i) // tiles_per_seq, j))))
            outs_s.append((jax.ShapeDtypeStruct((ts, n), F32), ss))
        else:
            dt = F32 if kind == "f32" else BF16
            outs_p.append((jax.ShapeDtypeStruct((tp, n), dt), sp))
            outs_s.append((jax.ShapeDtypeStruct((ts, n), dt), ss))
    return outs_p, outs_s


def _store_proj(y, out_kinds, outs, cols=slice(None), scale=1.0):
    for kind, o_ref in zip(out_kinds, outs):
        if kind == "tail":
            o_ref[:, cols] = y[y.shape[0] - o_ref.shape[0]:]
        elif kind == "bf16":
            o_ref[:, cols] = (y * scale).astype(BF16) if scale != 1.0 else y.astype(BF16)
        else:
            o_ref[:, cols] = y


def _proj_act(xnp, xns, wt, layer, row0, n, *, tm, tn, act, out_kinds, bias=None, valid_rows=None, tail_rows=None,
              tiles_per_seq=1, side_casts=(), name):
    (tp, k), ts = xnp.shape, xns.shape[0]
    npt = tp // tm
    has_bias = bias is not None

    def pre(shared, scratch):
        _load_weight(shared[0], scratch[0], valid_rows=valid_rows)

    def body(shared, ins, outs, scratch, group):
        acc = _dot_nt(ins[0][...], scratch[0][...])
        if has_bias:
            acc = acc + shared[1][...]
        _store_proj(act(acc), out_kinds, outs)

    shared = [(wt, _wspec(k, tn, layer, row0))]
    if has_bias:
        shared.append((bias.reshape(1, n), pl.BlockSpec((1, tn), lambda j, i: (0, j))))
    xp_spec, xs_spec = _row_specs(tm, ts, k, col_of=lambda j: 0)
    outs_p, outs_s = _proj_outs(out_kinds, tp, ts, n, tm, tn, npt, tail_rows, tiles_per_seq)
    grid = (n // tn, npt + 1)
    side, side_outs = None, []
    if side_casts:
        assert grid[0] * grid[1] >= SIDE_CAST_CHUNKS
        chunk = lambda j, i: jnp.minimum(j * grid[1] + i, SIDE_CAST_CHUNKS - 1)
        n_main = len(shared)
        for w in side_casts:
            _, r, c = w.shape
            cr = r // SIDE_CAST_CHUNKS
            shared.append((w, pl.BlockSpec((1, cr, c), lambda j, i: (layer, chunk(j, i), 0))))
            side_outs.append((jax.ShapeDtypeStruct((r, c), BF16), pl.BlockSpec((cr, c), lambda j, i: (chunk(j, i), 0))))

        def side(shared_refs, out_refs):
            for w_ref, o_ref in zip(shared_refs[n_main:], out_refs):
                o_ref[...] = w_ref[0].astype(BF16)

    return _two_group_call(body, pre, grid=grid, row_axis=1, shared=shared,
                           ins_p=[(xnp, xp_spec)], ins_s=[(xns, xs_spec)], outs_p=outs_p, outs_s=outs_s,
                           scratch_shapes=[pltpu.VMEM((tn, k), BF16)], vmem_mib=56, name=name, side=side,
                           side_outs=side_outs)


def _proj_headnorm(xnp, xns, wt, layer, row0, n, g, *, tm, tn, scale, out_kinds, tail_rows=None, tiles_per_seq=1,
                   name):
    (tp, k), ts = xnp.shape, xns.shape[0]
    npt = tp // tm

    def pre(shared, scratch):
        _load_weight(shared[0], scratch[0])

    def body(shared, ins, outs, scratch, group):
        acc = _dot_nt(ins[0][...], scratch[0][...])
        gain = shared[1][...]
        for h in range(tn // HEAD_DIM):
            sl = slice(h * HEAD_DIM, (h + 1) * HEAD_DIM)
            blk = acc[:, sl]
            ms = jnp.mean(blk * blk, axis=-1, keepdims=True)
            _store_proj(blk * lax.rsqrt(ms + EPS) * gain, out_kinds, outs, cols=sl, scale=scale)

    shared = [(wt, _wspec(k, tn, layer, row0)),
              (g.reshape(1, HEAD_DIM), pl.BlockSpec((1, HEAD_DIM), lambda j, i: (0, 0)))]
    xp_spec, xs_spec = _row_specs(tm, ts, k, col_of=lambda j: 0)
    outs_p, outs_s = _proj_outs(out_kinds, tp, ts, n, tm, tn, npt, tail_rows, tiles_per_seq)
    return _two_group_call(body, pre, grid=(n // tn, npt + 1), row_axis=1, shared=shared,
                           ins_p=[(xnp, xp_spec)], ins_s=[(xns, xs_spec)], outs_p=outs_p, outs_s=outs_s,
                           scratch_shapes=[pltpu.VMEM((tn, k), BF16)], vmem_mib=56, name=name)


CONV_SUB = 256


def _causal_conv(u, cs, cw_ref, p_ref, carry_ref, st_ref, *, width, nseg, tiles_per_seq):
    tm, tn = u.shape
    seg_len = tm // nseg
    row8 = lax.broadcasted_iota(jnp.int32, (SUBLANES, tn), 0)
    outs = []
    for s in range(nseg):
        seg = u[s * seg_len:(s + 1) * seg_len]
        prev8 = carry_ref[:, cs] if tiles_per_seq > 1 else p_ref[s, :, cs]
        acc = cw_ref[width - 1:width, cs] * seg
        for k in range(1, width):
            sh = pltpu.roll(seg, k, 0)
            first8 = jnp.where(row8 < k, pltpu.roll(prev8, k, 0), sh[0:SUBLANES])
            shk = jnp.concatenate([first8, sh[SUBLANES:]], axis=0)
            acc = acc + cw_ref[width - 1 - k:width - k, cs] * shk
        outs.append(acc)
        st_ref[s, :, cs] = seg[seg_len - SUBLANES:seg_len]
    if tiles_per_seq > 1:
        carry_ref[:, cs] = u[tm - SUBLANES:tm]
    return outs[0] if nseg == 1 else jnp.concatenate(outs, axis=0)


def _proj_conv(kind, xnp, xns, wt, layer, row0s, n, conv_w8, conv_b, prefix_p, prefix_s, *, tm, tn, seq_p, seq_s,
               out_dtype, name):
    (tp, k), ts = xnp.shape, xns.shape[0]
    npt = tp // tm
    tps_p = seq_p // tm
    nseg_s = ts // seq_s
    assert tps_p >= 1 and seq_p % tm == 0 and ts % seq_s == 0
    width = SSM_CONV if kind == "xbc" else SC_WIDTH
    nw = len(row0s)
    has_b = conv_b is not None

    def pre(shared, scratch):
        for slot in range(nw):
            _load_weight(shared[slot], scratch[1], slot=slot)

    def body(shared, ins, outs, scratch, group):
        x_ref, p_ref = ins
        o_ref, st_ref = outs
        carry_ref, w16_ref = scratch
        cw_ref = shared[nw]
        nseg, tps = (1, tps_p) if group == 0 else (nseg_s, 1)
        if tps > 1:
            @pl.when(lax.rem(_prompt_tile(pl.program_id(1)), tps) == 0)
            def _():
                carry_ref[...] = p_ref[0]
        x = x_ref[...]
        for c in range(tn // CONV_SUB):
            cs = slice(c * CONV_SUB, (c + 1) * CONV_SUB)
            if kind == "xbc":
                u = _dot_nt(x, w16_ref[0, cs, :])
            else:
                u = _dot_nt(x, w16_ref[1, cs, :]) * _dot_nt(x, w16_ref[2, cs, :])
            y = _causal_conv(u, cs, cw_ref, p_ref, carry_ref, st_ref, width=width, nseg=nseg, tiles_per_seq=tps)
            if kind == "xbc":
                y = y + shared[nw + 1][:, cs]
                o_ref[:, cs] = y * jax.nn.sigmoid(y)
            else:
                o_ref[:, cs] = (_dot_nt(x, w16_ref[0, cs, :]) * y).astype(o_ref.dtype)

    shared = [(wt, _wspec(k, tn, layer, r)) for r in row0s]
    shared.append((conv_w8, pl.BlockSpec((SUBLANES, tn), lambda j, i: (0, j))))
    if has_b:
        shared.append((conv_b.reshape(1, n), pl.BlockSpec((1, tn), lambda j, i: (0, j))))
    xp_spec, xs_spec = _row_specs(tm, ts, k, col_of=lambda j: 0)
    yp_spec, ys_spec = _row_specs(tm, ts, tn)
    ins_p = [(xnp, xp_spec),
             (prefix_p, pl.BlockSpec((1, SUBLANES, tn), lambda j, i: (_prompt_tile(i) // tps_p, 0, j)))]
    ins_s = [(xns, xs_spec), (prefix_s, pl.BlockSpec((nseg_s, SUBLANES, tn), lambda j, i: (0, 0, j)))]
    outs_p = [(jax.ShapeDtypeStruct((tp, n), out_dtype), yp_spec),
              (jax.ShapeDtypeStruct((npt, SUBLANES, n), F32),
               pl.BlockSpec((1, SUBLANES, tn), lambda j, i: (_prompt_tile(i), 0, j)))]
    outs_s = [(jax.ShapeDtypeStruct((ts, n), out_dtype), ys_spec),
              (jax.ShapeDtypeStruct((nseg_s, SUBLANES, n), F32),
               pl.BlockSpec((nseg_s, SUBLANES, tn), lambda j, i: (0, 0, j)))]
    (yp, tails_p), (ys, tails_s) = _two_group_call(
        body, pre, grid=(n // tn, npt + 1), row_axis=1, shared=shared, ins_p=ins_p, ins_s=ins_s, outs_p=outs_p,
        outs_s=outs_s, scratch_shapes=[pltpu.VMEM((SUBLANES, tn), F32), pltpu.VMEM((nw, tn, k), BF16)],
        vmem_mib=56, name=name)
    return (yp, tails_p[tps_p - 1::tps_p]), (ys, tails_s)


def _ssd_kernel(x_ref, b_ref, c_ref, dt_ref, zs_ref, alog_ref, dx_ref, ng_ref, ech_ref, els_ref, h0_ref,
                y_ref, hl_ref, ht_ref, *, L, cps, nsteps):
    ci = pl.program_id(1)
    hp = LANES // L
    ntiles = SSM_HEADS // hp
    gw = SSM_GROUP_DIM
    tw = hp * SSM_HEAD_DIM
    log2_l = L.bit_length() - 1

    @pl.when(ci == 0)
    def _():
        ht_ref[...] = h0_ref[0].T

    a = -jnp.exp(alog_ref[...])
    ri = lax.broadcasted_iota(jnp.int32, (L, L), 0)
    cj = lax.broadcasted_iota(jnp.int32, (L, L), 1)
    tri = (ri >= cj).astype(BF16)
    lane_blk = jnp.right_shift(lax.broadcasted_iota(jnp.int32, (1, LANES), 1), log2_l)
    row_l = lax.broadcasted_iota(jnp.int32, (L, LANES), 0)
    lane_s = jnp.bitwise_and(lax.broadcasted_iota(jnp.int32, (L, LANES), 1), L - 1)
    causal = row_l >= lane_s
    rb = jnp.right_shift(lax.broadcasted_iota(jnp.int32, (LANES, tw), 0), log2_l)
    cb_ = jnp.right_shift(lax.broadcasted_iota(jnp.int32, (LANES, tw), 1), SSM_HEAD_DIM.bit_length() - 1)
    blockdiag = rb == cb_
    tile_rows = lambda v: jnp.concatenate([v] * hp, axis=0)
    zpad = jnp.zeros((LANES - L, gw), F32)

    for r in range(cps):
        rows = slice(r * L, (r + 1) * L)
        x = x_ref[rows, :]
        bm = b_ref[rows, :]
        cm = c_ref[rows, :]
        dt = dt_ref[rows, :]
        da = dt * a

        acum = _dot_exact_rhs(tri, da, 3)
        eacum = jnp.exp(acum)
        dend = jnp.exp(acum[L - 1:L, :] - acum)
        w = dt * dend

        ech = ech_ref[...]
        if L == SSM_HEAD_DIM:
            cexp = jnp.concatenate(
                [jnp.broadcast_to(acum[:, h:h + 1], (L, SSM_HEAD_DIM)) for h in range(SSM_HEADS)], axis=1)
        else:
            (cexp,) = _dot_exact_lhs([acum], els_ref[...], [3])
        wx, ex = _dot_exact_lhs([w, eacum], ech, [2, 2])

        acum_t = tile_rows(acum).T
        dt_t = tile_rows(dt).T

        bsq = [tile_rows(bm[:, g * SSM_STATE:(g + 1) * SSM_STATE]) for g in range(SSM_GROUPS)]
        cbt = [_dot_nt(cm[:, g * SSM_STATE:(g + 1) * SSM_STATE].astype(BF16), bsq[g].astype(BF16))
               for g in range(SSM_GROUPS)]

        yd = []
        for t in range(ntiles):
            h_first = t * hp
            g = h_first // (SSM_HEADS // SSM_GROUPS)
            r_row = acum_t[h_first:h_first + 1, :]
            d_row = dt_t[h_first:h_first + 1, :]
            for jj in range(1, hp):
                sel = lane_blk == jj
                r_row = jnp.where(sel, acum_t[h_first + jj:h_first + jj + 1, :], r_row)
                d_row = jnp.where(sel, dt_t[h_first + jj:h_first + jj + 1, :], d_row)
            diff = cexp[:, t * LANES:(t + 1) * LANES] - r_row
            dec = jnp.exp(jnp.where(causal, diff, -jnp.inf))
            sc = (cbt[g] * dec * d_row).astype(BF16)
            xs = tile_rows(x[:, t * tw:(t + 1) * tw])
            rhs = jnp.where(blockdiag, xs, 0.0).astype(BF16)
            yd.append(_dot(sc, rhs))
        y = jnp.concatenate(yd, axis=1)

        xw = x * wx
        for g in range(SSM_GROUPS):
            gs = slice(g * gw, (g + 1) * gw)
            h_in = ht_ref[:, gs]
            y_off = _dot(cm[:, g * SSM_STATE:(g + 1) * SSM_STATE].astype(BF16), h_in.astype(BF16))
            yg = y[:, gs] + y_off * ex[:, gs] + dx_ref[:, gs] * x[:, gs]
            yg = yg * zs_ref[rows, gs]
            ms = jnp.mean(yg * yg, axis=-1, keepdims=True)
            y_ref[rows, gs] = (yg * lax.rsqrt(ms + EPS) * ng_ref[:, gs]).astype(y_ref.dtype)
            bm_t = bsq[g].T.astype(BF16)
            upd = jnp.concatenate([xw[:, gs], zpad], axis=0).astype(BF16)
            ht_ref[:, gs] = h_in * ex[L - 1:L, gs] + _dot(bm_t, upd)

    @pl.when(ci == nsteps - 1)
    def _():
        hl_ref[0] = ht_ref[...].T


def _ssd(xbc_act, dt, zs, a_log128, d_x, norm_g, ech, els, h0, *, L, cps, seq_len):
    t = xbc_act.shape[0]
    nseq = h0.shape[0]
    rows = L * cps
    ns = seq_len // rows
    rmap = lambda b, c: (b * ns + c, 0)
    cmap = lambda b, c: (0, 0)
    nb = SSM_INNER // SSM_BC
    return pl.pallas_call(
        functools.partial(_ssd_kernel, L=L, cps=cps, nsteps=ns),
        grid=(nseq, ns),
        in_specs=[pl.BlockSpec((rows, SSM_INNER), rmap),
                  pl.BlockSpec((rows, SSM_BC), lambda b, c: (b * ns + c, nb)),
                  pl.BlockSpec((rows, SSM_BC), lambda b, c: (b * ns + c, nb + 1)),
                  pl.BlockSpec((rows, LANES), rmap),
                  pl.BlockSpec((rows, SSM_INNER), rmap),
                  pl.BlockSpec((1, LANES), cmap),
                  pl.BlockSpec((1, SSM_INNER), cmap),
                  pl.BlockSpec((1, SSM_INNER), cmap),
                  pl.BlockSpec(ech.shape, cmap),
                  pl.BlockSpec(els.shape, cmap),
                  pl.BlockSpec((1, SSM_INNER, SSM_STATE), lambda b, c: (b, 0, 0))],
        out_specs=[pl.BlockSpec((rows, SSM_INNER), rmap),
                   pl.BlockSpec((1, SSM_INNER, SSM_STATE), lambda b, c: (b, 0, 0))],
        out_shape=[jax.ShapeDtypeStruct((t, SSM_INNER), BF16),
                   jax.ShapeDtypeStruct((nseq, SSM_INNER, SSM_STATE), F32)],
        scratch_shapes=[pltpu.VMEM((SSM_STATE, SSM_INNER), F32)],
        compiler_params=_cparams(("arbitrary", "arbitrary"), 48),
        name=f"ssd_L{L}",
    )(xbc_act, xbc_act, xbc_act, dt, zs, a_log128, d_x, norm_g, ech, els, h0)


def _softmax_pv(s_parts, v_parts):
    m = functools.reduce(jnp.maximum, [jnp.max(s, axis=1, keepdims=True) for s in s_parts])
    l = None
    o = None
    for s, v in zip(s_parts, v_parts):
        p = jnp.exp(s - m)
        ls = jnp.sum(p, axis=1, keepdims=True)
        os_ = _dot(p.astype(BF16), v)
        l = ls if l is None else l + ls
        o = os_ if o is None else o + os_
    return o / l


def _attn_prompt_kernel(q_ref, k0_ref, k1_ref, k2_ref, v0_ref, v1_ref, v2_ref, bias_ref, o_ref):
    i = pl.program_id(1)
    k_refs = (k0_ref, k1_ref, k2_ref)
    v_refs = (v0_ref, v1_ref, v2_ref)

    first_valid = (ATT_K_BLOCKS - 1 - i) * ATT_Q_BLOCK
    kidx = lax.broadcasted_iota(jnp.int32, (ATT_Q_BLOCK, ATT_Q_BLOCK), 1)
    for h in range(N_HEADS):
        sl = slice(h * HEAD_DIM, (h + 1) * HEAD_DIM)
        q = q_ref[:, sl]
        s_parts = []
        for kb in range(ATT_K_BLOCKS):
            s = _dot_nt(q, k_refs[kb][:, sl]) + bias_ref[h, :, kb * ATT_Q_BLOCK:(kb + 1) * ATT_Q_BLOCK]
            s_parts.append(jnp.where(kidx + kb * ATT_Q_BLOCK >= first_valid, s, NEG_INF))
        o = _softmax_pv(s_parts, [v_refs[kb][:, sl] for kb in range(ATT_K_BLOCKS)])
        o_ref[:, sl] = o.astype(o_ref.dtype)


def _attn_prompt(q, k, v, bias, *, nseq, seq_len):
    t = q.shape[0]
    nqb = seq_len // ATT_Q_BLOCK
    qmap = lambda b, i: (b * nqb + i, 0)

    def kmap(back):
        return lambda b, i: (b * nqb + jnp.maximum(i - back, 0), 0)

    blk = (ATT_Q_BLOCK, ATT_DIM)
    kv_specs = [pl.BlockSpec(blk, kmap(ATT_K_BLOCKS - 1 - kb)) for kb in range(ATT_K_BLOCKS)]
    return pl.pallas_call(
        _attn_prompt_kernel,
        grid=(nseq, nqb),
        in_specs=[pl.BlockSpec(blk, qmap)] + kv_specs + kv_specs
        + [pl.BlockSpec(bias.shape, lambda b, i: (0, 0, 0))],
        out_specs=pl.BlockSpec(blk, qmap),
        out_shape=jax.ShapeDtypeStruct((t, ATT_DIM), BF16),
        compiler_params=_cparams(("arbitrary", "arbitrary"), 56),
        name="attn_prompt",
    )(q, k, k, k, v, v, v, bias)


def _heads_to_rows(ref):
    return jnp.concatenate([ref[:, h * HEAD_DIM:(h + 1) * HEAD_DIM] for h in range(N_HEADS)], axis=0)


def _attn_sample_kernel(q_ref, kn_ref, vn_ref, kc_ref, vc_ref, bf_ref, bc_ref, px_ref, bn_ref, o_ref):
    t = q_ref.shape[0]
    q2 = _heads_to_rows(q_ref)
    kn2, vn2 = _heads_to_rows(kn_ref), _heads_to_rows(vn_ref)
    npos = SAMPLE_KEY_TILE // N_HEADS
    ntiles = kc_ref.shape[2] // npos
    nfar = ntiles - bc_ref.shape[1] // npos
    assert t & (t - 1) == 0
    row_head = jnp.right_shift(lax.broadcasted_iota(jnp.int32, (N_HEADS * t, SAMPLE_KEY_TILE), 0),
                               t.bit_length() - 1)
    col_head = jnp.bitwise_and(lax.broadcasted_iota(jnp.int32, (N_HEADS * t, SAMPLE_KEY_TILE), 1), N_HEADS - 1)
    same_head = row_head == col_head
    far_bias = jnp.where(same_head, bf_ref[...], NEG_INF)
    s_parts, v_parts = [], []
    for c in range(ntiles):
        ps = slice(c * npos, (c + 1) * npos)
        kc = kc_ref[0, 0, ps, :, :].reshape(SAMPLE_KEY_TILE, HEAD_DIM)
        vc = vc_ref[0, 0, ps, :, :].reshape(SAMPLE_KEY_TILE, HEAD_DIM)
        if c < nfar:
            bias = far_bias
        else:
            (near,) = _dot_exact_lhs([bc_ref[:, (c - nfar) * npos:(c - nfar + 1) * npos]], px_ref[...], [3])
            bias = jnp.where(same_head, near, NEG_INF)
        s_parts.append(_dot_nt(q2, kc.astype(BF16)) + bias)
        v_parts.append(vc.astype(BF16))
    s_parts.append(_dot_nt(q2, kn2) + bn_ref[...])
    v_parts.append(vn2)
    o2 = _softmax_pv(s_parts, v_parts)
    for h in range(N_HEADS):
        o_ref[:, h * HEAD_DIM:(h + 1) * HEAD_DIM] = o2[h * t:(h + 1) * t].astype(o_ref.dtype)


def _attn_sample(q, k, v, k_cache, v_cache, layer, bias_far, bias_c, pos_expand, bias_n, *, nseq, seq_len):
    t = q.shape[0]
    blk = (seq_len, ATT_DIM)
    rmap = lambda b: (b, 0)
    const = lambda b: (0, 0)
    cspec = pl.BlockSpec((1, 1) + k_cache.shape[2:], lambda b: (layer, b, 0, 0, 0))
    return pl.pallas_call(
        _attn_sample_kernel,
        grid=(nseq,),
        in_specs=[pl.BlockSpec(blk, rmap), pl.BlockSpec(blk, rmap), pl.BlockSpec(blk, rmap), cspec, cspec,
                  pl.BlockSpec(bias_far.shape, const), pl.BlockSpec(bias_c.shape, const),
                  pl.BlockSpec(pos_expand.shape, const), pl.BlockSpec(bias_n.shape, const)],
        out_specs=pl.BlockSpec(blk, rmap),
        out_shape=jax.ShapeDtypeStruct((t, ATT_DIM), BF16),
        compiler_params=_cparams(("arbitrary",), 56),
        name="attn_sample",
    )(q, k, v, k_cache, v_cache, bias_far, bias_c, pos_expand, bias_n)


def _merge(ys_p, ys_s, w_ssm, w_sc, w_att, layer, gates_p, gates_s, *, tm, tn):
    (tp, k), ts = ys_p[0].shape, ys_s[0].shape[0]
    n = w_ssm.shape[2]
    ncol, npt = n // tn, tp // tm

    def pre(shared, scratch):
        @pl.when(pl.program_id(1) == 0)
        def _():
            for b in range(N_BRANCH):
                scratch[0][b] = shared[b][0].astype(BF16)

    def body(shared, ins, outs, scratch, group):
        w16_ref = scratch[0]
        m = ins[3][...] * _dot(ins[0][...], w16_ref[0])
        m = m + ins[4][...] * _dot(ins[1][...], w16_ref[1])
        m = m + ins[5][...] * _dot(ins[2][...], w16_ref[2])
        outs[0][...] = m.astype(outs[0].dtype)

    rhs = pl.BlockSpec((1, k, tn), lambda j, i: (layer, 0, j))
    lp, ls = _row_specs(tm, ts, k, col_of=lambda j: 0)
    ins_p = [(y, lp) for y in ys_p]
    ins_s = [(y, ls) for y in ys_s]
    for b in range(N_BRANCH):
        gp, gs = _row_specs(tm, ts, tn, col_of=lambda j, b=b: b * ncol + j)
        ins_p.append((gates_p, gp))
        ins_s.append((gates_s, gs))
    op, os_ = _row_specs(tm, ts, tn)
    (mp,), (ms,) = _two_group_call(
        body, pre, grid=(ncol, npt + 1), row_axis=1, shared=[(w_ssm, rhs), (w_sc, rhs), (w_att, rhs)],
        ins_p=ins_p, ins_s=ins_s, outs_p=[(jax.ShapeDtypeStruct((tp, n), BF16), op)],
        outs_s=[(jax.ShapeDtypeStruct((ts, n), BF16), os_)],
        scratch_shapes=[pltpu.VMEM((N_BRANCH, k, tn), BF16)], vmem_mib=56, name="merge")
    return mp, ms


def _wo_body(shared, ins, outs, scratch, group):
    h = ins[0][...] + _dot(ins[1][...], shared[0][...])
    outs[0][...] = h
    ms = jnp.mean(h * h, axis=-1, keepdims=True)
    outs[1][...] = (h * lax.rsqrt(ms + EPS) * shared[1][...]).astype(outs[1].dtype)


def _wo(xp, xs, mp, ms, w_o, g, *, tm):
    (tp, d), ts = xp.shape, xs.shape[0]
    npt = tp // tm
    sp, ss = _row_specs(tm, ts, d, two_d_grid=False)
    const = lambda i: (0, 0)
    (hp, hnp), (hs, hns) = _two_group_call(
        _wo_body, None, grid=(npt + 1,), row_axis=0,
        shared=[(w_o, pl.BlockSpec((d, d), const)), (g.reshape(1, d), pl.BlockSpec((1, d), const))],
        ins_p=[(xp, sp), (mp, sp)], ins_s=[(xs, ss), (ms, ss)],
        outs_p=[(jax.ShapeDtypeStruct((tp, d), F32), sp), (jax.ShapeDtypeStruct((tp, d), BF16), sp)],
        outs_s=[(jax.ShapeDtypeStruct((ts, d), F32), ss), (jax.ShapeDtypeStruct((ts, d), BF16), ss)],
        scratch_shapes=[], vmem_mib=52, name="wo")
    return (hp, hnp), (hs, hns)


def _ffn_body(shared, ins, outs, scratch, group):
    c = pl.program_id(1)
    a = _dot(ins[1][...], shared[0][...])
    a = jnp.square(jnp.maximum(a, 0.0)).astype(BF16)
    contrib = _dot(a, shared[1][...])

    @pl.when(c == 0)
    def _():
        outs[0][...] = ins[0][...] + contrib

    @pl.when(c > 0)
    def _():
        outs[0][...] += contrib

    if len(outs) > 1:
        @pl.when(c == pl.num_programs(1) - 1)
        def _():
            y = outs[0][...]
            ms = jnp.mean(y * y, axis=-1, keepdims=True)
            outs[1][...] = (y * lax.rsqrt(ms + EPS) * shared[2][...]).astype(outs[1].dtype)


def _ffn(hp, hnp, hs, hns, w1, w2, *, tm, tc, next_norm_g=None):
    (tp, d), ts = hp.shape, hs.shape[0]
    dff = w1.shape[1]
    npt = tp // tm
    sp, ss = _row_specs(tm, ts, d, two_d_grid=False)
    shared = [(w1, pl.BlockSpec((d, tc), lambda i, c: (0, c))), (w2, pl.BlockSpec((tc, d), lambda i, c: (c, 0)))]
    outs_p = [(jax.ShapeDtypeStruct((tp, d), F32), sp)]
    outs_s = [(jax.ShapeDtypeStruct((ts, d), F32), ss)]
    if next_norm_g is not None:
        shared.append((next_norm_g.reshape(1, d), pl.BlockSpec((1, d), lambda i, c: (0, 0))))
        outs_p.append((jax.ShapeDtypeStruct((tp, d), BF16), sp))
        outs_s.append((jax.ShapeDtypeStruct((ts, d), BF16), ss))
    return _two_group_call(
        _ffn_body, None, grid=(npt + 1, dff // tc), row_axis=0, shared=shared,
        ins_p=[(hp, sp), (hnp, sp)], ins_s=[(hs, ss), (hns, ss)], outs_p=outs_p, outs_s=outs_s,
        scratch_shapes=[], vmem_mib=52, name="ffn")


def _head_expand(lanes_per_head):
    rows = lax.broadcasted_iota(jnp.int32, (LANES, SSM_HEADS * lanes_per_head), 0)
    cols = lax.broadcasted_iota(jnp.int32, (LANES, SSM_HEADS * lanes_per_head), 1) // lanes_per_head
    return (rows == cols).astype(BF16)


def _bias_tables_kernel(row_ref, toe_ref, pb_ref):
    x = jnp.broadcast_to(row_ref[0], (ATT_Q_BLOCK, TOEPLITZ_COLS))
    t = pltpu.roll(x, 0, 1, stride=1, stride_axis=0)
    toe_ref[0] = t[:toe_ref.shape[1]]
    qi = lax.broadcasted_iota(jnp.int32, (ATT_Q_BLOCK, ATT_WINDOW), 0)
    kj = lax.broadcasted_iota(jnp.int32, (ATT_Q_BLOCK, ATT_WINDOW), 1)
    band = kj - jnp.bitwise_and(qi, -CHUNK)
    pb_ref[0] = jnp.where((band >= 0) & (band < ATT_BAND), t[:, :ATT_WINDOW], NEG_INF)


def _bias_tables(rel_bias, sample_rows):
    tbl = rel_bias.astype(F32)
    far = jnp.broadcast_to(tbl[:, 2 * MAX_REL:], (N_HEADS, ATT_PAST - MAX_REL))
    wrap = jnp.broadcast_to(tbl[:, 2 * MAX_REL:], (N_HEADS, ATT_Q_BLOCK - 1))
    n_near = TOEPLITZ_COLS - far.shape[1] - tbl.shape[1] - wrap.shape[1]
    assert ATT_WINDOW <= far.shape[1] + tbl.shape[1] + n_near
    near = jnp.broadcast_to(tbl[:, :1], (N_HEADS, n_near))
    row = jnp.concatenate([far, tbl[:, ::-1], near, wrap], axis=1).reshape(N_HEADS, 1, TOEPLITZ_COLS)
    return pl.pallas_call(
        _bias_tables_kernel,
        grid=(N_HEADS,),
        in_specs=[pl.BlockSpec((1, 1, TOEPLITZ_COLS), lambda h: (h, 0, 0))],
        out_specs=[pl.BlockSpec((1, sample_rows, TOEPLITZ_COLS), lambda h: (h, 0, 0)),
                   pl.BlockSpec((1, ATT_Q_BLOCK, ATT_WINDOW), lambda h: (h, 0, 0))],
        out_shape=[jax.ShapeDtypeStruct((N_HEADS, sample_rows, TOEPLITZ_COLS), F32),
                   jax.ShapeDtypeStruct((N_HEADS, ATT_Q_BLOCK, ATT_WINDOW), F32)],
        compiler_params=_cparams(("arbitrary",), 32),
        name="bias_tables",
    )(row)


def _sample_bias(toeplitz, t, lc):
    npos = SAMPLE_KEY_TILE // N_HEADS
    far = (lc - MAX_REL) // npos * npos
    same = jnp.arange(N_HEADS)[:, None] == jnp.arange(N_HEADS)[None, :]
    bias_far = jnp.broadcast_to(toeplitz[:, :1, :1], (N_HEADS, t, 1))
    bias_c = toeplitz[:, :t, far:lc].reshape(N_HEADS * t, lc - far)
    pos_expand = (jnp.arange(npos)[:, None] == jnp.arange(SAMPLE_KEY_TILE)[None, :] // N_HEADS).astype(BF16)
    tn_ = toeplitz[:, :t, lc:lc + t]
    bias_n = jnp.where(same[:, None, :, None], tn_[:, :, None, :], NEG_INF)
    return (bias_far.reshape(N_HEADS * t, 1), bias_c, pos_expand, bias_n.reshape(N_HEADS * t, N_HEADS * t))


def _pad_rows_to8(a, axis):
    pad = [(0, 0)] * a.ndim
    pad[axis] = (SUBLANES - a.shape[axis], 0)
    return jnp.pad(a, pad)


def _layer(xp, xs, xn, lw, layer, gp, gs):
    tm, tn = PROMPT_ROW_TILE, 1024
    wt = lw["w_in_t"]
    off = IN_OFFSETS
    seq_p, seq_s = gp["seq_len"], gs["seq_len"]
    tiles_per_seq = seq_p // tm
    tail_rows = min(tm, ATT_PAST, seq_p)

    xnp, xns = xn if xn is not None else _rmsnorm(xp, xs, lw["norm_mix_g"], tm)
    proj = functools.partial(_proj_act, xnp, xns, wt, layer, tm=tm)
    (zp,), (zs,) = proj(off[0], SSM_INNER, tn=tn, act=lambda a: a * jax.nn.sigmoid(a), out_kinds=["f32"],
                        name="proj_z")
    (dtp,), (dts,) = proj(off[2], LANES, tn=LANES, act=jax.nn.softplus, out_kinds=["f32"], bias=lw["dt_bias128"],
                          valid_rows=SSM_HEADS, name="proj_dt")
    (gtp,), (gts,), (ffn_w1, ffn_w2) = proj(off[9], N_BRANCH * D_MODEL, tn=tn, act=jax.nn.sigmoid,
                                            out_kinds=["f32"], side_casts=(lw["ffn_w1"], lw["ffn_w2"]),
                                            name="proj_gates")
    (v16p, vtp), (v16s, vts) = proj(off[8], ATT_DIM, tn=tn, act=lambda a: a, out_kinds=["bf16", "tail"],
                                    tail_rows=tail_rows, tiles_per_seq=tiles_per_seq, name="proj_v")
    (q16p,), (q16s,) = _proj_headnorm(xnp, xns, wt, layer, off[6], ATT_DIM, lw["q_norm_g"], tm=tm, tn=tn,
                                      scale=HEAD_DIM ** -0.5, out_kinds=["bf16"], name="proj_q")
    (k16p, ktp), (k16s, kts) = _proj_headnorm(xnp, xns, wt, layer, off[7], ATT_DIM, lw["k_norm_g"], tm=tm, tn=tn,
                                              scale=1.0, out_kinds=["bf16", "tail"], tail_rows=tail_rows,
                                              tiles_per_seq=tiles_per_seq, name="proj_k")
    (xbcp, convp), (xbcs, convs) = _proj_conv(
        "xbc", xnp, xns, wt, layer, [off[1]], SSM_CONV_DIM, lw["ssm_conv_w8"], lw["ssm_conv_b"],
        gp["ssm_conv_prefix"], gs["ssm_conv_prefix"], tm=tm, tn=512, seq_p=seq_p, seq_s=seq_s, out_dtype=F32,
        name="proj_xbc")
    (yscp, scp), (yscs, scs) = _proj_conv(
        "sc", xnp, xns, wt, layer, [off[3], off[4], off[5]], SC_DIM, lw["sc_conv_w8"], None,
        gp["sc_prefix"], gs["sc_prefix"], tm=tm, tn=512, seq_p=seq_p, seq_s=seq_s, out_dtype=BF16, name="proj_sc")

    ssd = functools.partial(_ssd, a_log128=lw["a_log128"], d_x=lw["d_x"], norm_g=lw["ssm_norm_g"], ech=lw["ech"])
    yssmp, hlp = ssd(xbcp, dtp, zp, els=lw["ech"], h0=gp["ssm_h0"], L=CHUNK,
                     cps=min(SSD_CHUNKS_PER_STEP, seq_p // CHUNK), seq_len=seq_p)
    yssms, hls = ssd(xbcs, dts, zs, els=_head_expand(seq_s), h0=gs["ssm_h0"], L=seq_s, cps=1, seq_len=seq_s)

    op = _attn_prompt(q16p, k16p, v16p, lw["prompt_bias"], nseq=gp["nseq"], seq_len=seq_p)
    os_ = _attn_sample(q16s, k16s, v16s, gs["kv_cache"][0], gs["kv_cache"][1], layer, *lw["sample_bias"],
                       nseq=gs["nseq"], seq_len=seq_s)

    mtm = 512
    mp, ms = _merge((yssmp, yscp, op), (yssms, yscs, os_), lw["ssm_out_w"], lw["sc_out_w"], lw["attn_out_w"],
                    layer, gtp, gts, tm=mtm, tn=512)
    (hp, hnp), (hs, hns) = _wo(xp, xs, mp, ms, lw["w_o"], lw["norm_ffn_g"], tm=mtm)
    fp, fs = _ffn(hp, hnp, hs, hns, ffn_w1, ffn_w2, tm=mtm, tc=1024, next_norm_g=lw.get("next_norm_g"))
    xn_next = (fp[1], fs[1]) if len(fp) > 1 else None
    return (fp[0], (ktp, vtp, hlp, convp, scp)), (fs[0], (kts, vts, hls, convs, scs)), xn_next


def kernel(x_prompt, x_sample, cache_attn_k, cache_attn_v, state_ssm, state_ssm_conv, state_short_conv,
           norm_mix_g, w_in, ssm_conv_w, ssm_conv_b, ssm_dt_bias, ssm_a_log, ssm_d, ssm_norm_g, ssm_out_w,
           sc_conv_w, sc_out_w, q_norm_g, k_norm_g, rel_bias, attn_out_w, w_o, norm_ffn_g, ffn_w1, ffn_w2):
    bp, lp, d = x_prompt.shape
    bs, ls, _ = x_sample.shape
    lc = cache_attn_k.shape[2]
    assert lc == ATT_PAST and lc + ls <= TOEPLITZ_COLS and ls <= ATT_Q_BLOCK
    ech = _head_expand(SSM_HEAD_DIM)
    w_in_t = jnp.swapaxes(w_in, 1, 2)

    yp = x_prompt.reshape(bp * lp, d)
    ys = x_sample.reshape(bs * ls, d)
    new_p, new_s = [], []
    xn = None
    for l in range(DEPTH):
        toeplitz, prompt_bias = _bias_tables(rel_bias[l], ls)
        lw = {
            "norm_mix_g": norm_mix_g[l], "norm_ffn_g": norm_ffn_g[l],
            "w_in_t": w_in_t,
            "dt_bias128": jnp.pad(ssm_dt_bias[l].astype(F32), (0, LANES - SSM_HEADS)),
            "a_log128": jnp.pad(ssm_a_log[l].astype(F32), (0, LANES - SSM_HEADS)).reshape(1, LANES),
            "d_x": jnp.repeat(ssm_d[l].astype(F32), SSM_HEAD_DIM).reshape(1, SSM_INNER),
            "ssm_norm_g": ssm_norm_g[l].astype(F32).reshape(1, SSM_INNER),
            "ssm_conv_w8": jnp.pad(ssm_conv_w[l].astype(F32), ((0, SUBLANES - SSM_CONV), (0, 0))),
            "ssm_conv_b": ssm_conv_b[l].astype(F32),
            "sc_conv_w8": jnp.pad(sc_conv_w[l].astype(F32), ((0, SUBLANES - SC_WIDTH), (0, 0))),
            "q_norm_g": q_norm_g[l].astype(F32), "k_norm_g": k_norm_g[l].astype(F32),
            "ssm_out_w": ssm_out_w, "sc_out_w": sc_out_w, "attn_out_w": attn_out_w,
            "w_o": _cast_bf16(w_o, l, tr=1024),
            "ffn_w1": ffn_w1, "ffn_w2": ffn_w2,
            "ech": ech,
            "prompt_bias": prompt_bias,
            "sample_bias": _sample_bias(toeplitz, ls, lc),
        }
        gp = dict(nseq=bp, seq_len=lp,
                  ssm_conv_prefix=jnp.zeros((bp, SUBLANES, SSM_CONV_DIM), F32),
                  ssm_h0=jnp.zeros((bp, SSM_INNER, SSM_STATE), F32),
                  sc_prefix=jnp.zeros((bp, SUBLANES, SC_DIM), F32))
        gs = dict(nseq=bs, seq_len=ls,
                  ssm_conv_prefix=_pad_rows_to8(state_ssm_conv[l].astype(F32), 1),
                  ssm_h0=state_ssm[l].astype(F32).reshape(bs, SSM_INNER, SSM_STATE),
                  sc_prefix=_pad_rows_to8(state_short_conv[l].astype(F32), 1),
                  kv_cache=(cache_attn_k, cache_attn_v))
        if l + 1 < DEPTH:
            lw["next_norm_g"] = norm_mix_g[l + 1]
        (yp, st_p), (ys, st_s), xn = _layer(yp, ys, xn, lw, l, gp, gs)
        new_p.append(st_p)
        new_s.append(st_s)

    keep = min(ATT_PAST, lp)

    def stack(states, fn):
        return jnp.stack([fn(s) for s in states])

    hshape = lambda b: (b, SSM_HEADS, SSM_HEAD_DIM, SSM_STATE)
    return (
        yp.reshape(bp, lp, d),
        ys.reshape(bs, ls, d),
        stack(new_p, lambda s: s[0].reshape(bp, keep, N_HEADS, HEAD_DIM)),
        stack(new_p, lambda s: s[1].reshape(bp, keep, N_HEADS, HEAD_DIM)),
        stack(new_s, lambda s: s[0].reshape(bs, ls, N_HEADS, HEAD_DIM)),
        stack(new_s, lambda s: s[1].reshape(bs, ls, N_HEADS, HEAD_DIM)),
        stack(new_p, lambda s: s[2].reshape(hshape(bp))),
        stack(new_s, lambda s: s[2].reshape(hshape(bs))),
        stack(new_p, lambda s: s[3][:, SUBLANES - (SSM_CONV - 1):]),
        stack(new_s, lambda s: s[3][:, SUBLANES - (SSM_CONV - 1):]),
        stack(new_p, lambda s: s[4][:, SUBLANES - (SC_WIDTH - 1):]),
        stack(new_s, lambda s: s[4][:, SUBLANES - (SC_WIDTH - 1):]),
    )
```
